```python
import math
import jax, jax.numpy as jnp
from jax import lax
import numpy as np

D_MODEL = 2048
BATCH = 16
SEQ = 2048
DEPTH = 1
DEC_BATCH = 2
DEC_SEQ = 16384
PAST_LEN = 128

MIX_WIDTH = D_MODEL
FOURIER_WIDTH = MIX_WIDTH // 2
FOURIER_GROUP = 128
N_FGROUPS = FOURIER_WIDTH // FOURIER_GROUP
ATTN_WIDTH = MIX_WIDTH - FOURIER_WIDTH
ATTN_HEAD = 128
N_HEADS = ATTN_WIDTH // ATTN_HEAD
HALF_DIM = ATTN_HEAD // 2
IN_WIDTH = FOURIER_WIDTH + 3 * ATTN_WIDTH
D_FF = ((8 * D_MODEL // 3 + 255) // 256) * 256
NUM_BUCKETS = 32
MAX_DISTANCE = 128
Q_BLOCK = 128
EPS = 1e-6

kernel_name = "hybrid_fnet_diffattn_encoder"


def rmsnorm(x, g):
    xf = x.astype(jnp.float32)
    y = xf * lax.rsqrt(jnp.mean(xf * xf, axis=-1, keepdims=True) + EPS)
    return (y * g.astype(jnp.float32)).astype(x.dtype)


def t5_bucket(rel):
    nb = NUM_BUCKETS // 2
    max_exact = nb // 2
    ret = (rel > 0).astype(jnp.int32) * nb
    n = jnp.abs(rel)
    nf = jnp.maximum(n, 1).astype(jnp.float32)
    large = max_exact + (jnp.log(nf / max_exact) / math.log(MAX_DISTANCE / max_exact)
                         * (nb - max_exact)).astype(jnp.int32)
    large = jnp.minimum(large, nb - 1)
    return ret + jnp.where(n < max_exact, n, large)


def fourier_mix(u, w_f):
    B, S, _ = u.shape
    ug = u.reshape(B, S, N_FGROUPS, FOURIER_GROUP).astype(jnp.float32)
    f = jnp.fft.fft2(ug, axes=(1, 3), norm="ortho").real
    y = jnp.einsum('bsgc,gce->bsge', f, w_f.astype(jnp.float32))
    return y.reshape(B, S, FOURIER_WIDTH).astype(u.dtype)


def diff_attention(q, k, v, lam, lam_init, subln_g, rel_bias):
    B, S = q.shape[0], q.shape[1]
    nblk = S // Q_BLOCK
    scale = HALF_DIM ** -0.5
    qb = q.reshape(B, nblk, Q_BLOCK, N_HEADS, 2, HALF_DIM).transpose(1, 0, 2, 3, 4, 5)
    starts = jnp.arange(nblk, dtype=jnp.int32) * Q_BLOCK
    kpos = jnp.arange(S, dtype=jnp.int32)
    kf = k.astype(jnp.float32)
    vf = v.astype(jnp.float32)
    table = rel_bias.astype(jnp.float32)

    def block(args):
        qi, start = args
        qpos = start + jnp.arange(Q_BLOCK, dtype=jnp.int32)
        bias = table[t5_bucket(kpos[None, :] - qpos[:, None])]
        bias = bias.transpose(2, 0, 1)
        s = jnp.einsum('bqhcd,bkhcd->bchqk', qi.astype(jnp.float32), kf) * scale
        p = jax.nn.softmax(s + bias[None, None], axis=-1)
        a = p[:, 0] - lam * p[:, 1]
        return jnp.einsum('bhqk,bkhe->bqhe', a, vf)

    o = lax.map(block, (qb, starts))
    o = o.transpose(1, 0, 2, 3, 4).reshape(B, S, N_HEADS, ATTN_HEAD)
    o = o * lax.rsqrt(jnp.mean(o * o, axis=-1, keepdims=True) + EPS) * subln_g.astype(jnp.float32)
    o = o * (1.0 - lam_init)
    return o.reshape(B, S, ATTN_WIDTH).astype(q.dtype)


def encoder(x, norm1_g, w_in, w_fourier, lambda_q1, lambda_k1, lambda_q2, lambda_k2,
            subln_g, w_out, norm2_g, w_gate, w_up, w_down, rel_bias, final_g):
    B, S, _ = x.shape
    for l in range(DEPTH):
        lam_init = 0.8 - 0.6 * math.exp(-0.3 * l)
        h = rmsnorm(x, norm1_g[l])
        proj = h @ w_in[l]
        u_f = proj[..., :FOURIER_WIDTH]
        q = proj[..., FOURIER_WIDTH:FOURIER_WIDTH + ATTN_WIDTH].reshape(B, S, N_HEADS, 2, HALF_DIM)
        k = proj[..., FOURIER_WIDTH + ATTN_WIDTH:FOURIER_WIDTH + 2 * ATTN_WIDTH].reshape(B, S, N_HEADS, 2, HALF_DIM)
        v = proj[..., FOURIER_WIDTH + 2 * ATTN_WIDTH:].reshape(B, S, N_HEADS, ATTN_HEAD)
        lam = (jnp.exp(jnp.sum(lambda_q1[l].astype(jnp.float32) * lambda_k1[l].astype(jnp.float32)))
               - jnp.exp(jnp.sum(lambda_q2[l].astype(jnp.float32) * lambda_k2[l].astype(jnp.float32)))
               + lam_init)
        y_f = fourier_mix(u_f, w_fourier[l])
        y_a = diff_attention(q, k, v, lam, lam_init, subln_g[l], rel_bias)
        x = x + jnp.concatenate([y_f, y_a], axis=-1) @ w_out[l]
        h2 = rmsnorm(x, norm2_g[l])
        x = x + (jax.nn.silu(h2 @ w_gate[l]) * (h2 @ w_up[l])) @ w_down[l]
    return rmsnorm(x, final_g)


def setup_inputs(seed: int = 0) -> dict:
    key = jax.random.key(seed)
    ks = jax.random.split(key, 20)
    f32 = jnp.float32
    nrm = lambda k, shape, s: (jax.random.normal(k, shape, f32) * s).astype(f32)
    return {
        "x_prompt": nrm(ks[0], (BATCH, SEQ, D_MODEL), 1.0),
        "x_sample": nrm(ks[1], (DEC_BATCH, DEC_SEQ, D_MODEL), 1.0),
        "norm1_g": 1.0 + nrm(ks[2], (DEPTH, D_MODEL), 0.02),
        "w_in": nrm(ks[3], (DEPTH, D_MODEL, IN_WIDTH), D_MODEL ** -0.5),
        "w_fourier": nrm(ks[4], (DEPTH, N_FGROUPS, FOURIER_GROUP, FOURIER_GROUP), FOURIER_GROUP ** -0.5),
        "lambda_q1": nrm(ks[5], (DEPTH, HALF_DIM), 0.1),
        "lambda_k1": nrm(ks[6], (DEPTH, HALF_DIM), 0.1),
        "lambda_q2": nrm(ks[7], (DEPTH, HALF_DIM), 0.1),
        "lambda_k2": nrm(ks[8], (DEPTH, HALF_DIM), 0.1),
        "subln_g": 1.0 + nrm(ks[9], (DEPTH, ATTN_HEAD), 0.02),
        "w_out": nrm(ks[10], (DEPTH, MIX_WIDTH, D_MODEL), MIX_WIDTH ** -0.5),
        "norm2_g": 1.0 + nrm(ks[11], (DEPTH, D_MODEL), 0.02),
        "w_gate": nrm(ks[12], (DEPTH, D_MODEL, D_FF), D_MODEL ** -0.5),
        "w_up": nrm(ks[13], (DEPTH, D_MODEL, D_FF), D_MODEL ** -0.5),
        "w_down": nrm(ks[14], (DEPTH, D_FF, D_MODEL), D_FF ** -0.5),
        "rel_bias": nrm(ks[15], (NUM_BUCKETS, N_HEADS), 0.5),
        "final_g": 1.0 + nrm(ks[16], (D_MODEL,), 0.02),
    }


def reference(x_prompt, x_sample, norm1_g, w_in, w_fourier, lambda_q1, lambda_k1, lambda_q2,
              lambda_k2, subln_g, w_out, norm2_g, w_gate, w_up, w_down, rel_bias, final_g):
    y_prompt = encoder(x_prompt, norm1_g, w_in, w_fourier, lambda_q1, lambda_k1, lambda_q2,
                       lambda_k2, subln_g, w_out, norm2_g, w_gate, w_up, w_down, rel_bias, final_g)
    y_sample = encoder(x_sample, norm1_g, w_in, w_fourier, lambda_q1, lambda_k1, lambda_q2,
                       lambda_k2, subln_g, w_out, norm2_g, w_gate, w_up, w_down, rel_bias, final_g)
    return (y_prompt, y_sample)
```

```python
import functools
import math

import numpy as np
import jax
import jax.numpy as jnp
from jax import lax
from jax.experimental import pallas as pl
from jax.experimental.pallas import tpu as pltpu

F32 = jnp.float32
BF16 = jnp.bfloat16

D_MODEL = 2048
FOURIER_WIDTH = 1024
FOURIER_GROUP = 128
N_FGROUPS = 8
ATTN_WIDTH = 1024
ATTN_HEAD = 128
N_HEADS = 8
HALF_DIM = 64
D_FF = 5632
NUM_BUCKETS = 32
MAX_DISTANCE = 128
EPS = 1e-6
LAM_INIT = 0.8 - 0.6 * math.exp(-0.3 * 0)

SEQ_MINOR = 128
TOKEN_TILE = 512
FF_TILE = 512
BIAS_TILES = 5
NEG_BIG = -1e30
VMEM_LIMIT = 56 * 1024 * 1024


def _cparams(sem):
    return pltpu.CompilerParams(dimension_semantics=sem, vmem_limit_bytes=VMEM_LIMIT)


def _dft128_tables():
    n = np.arange(FOURIER_GROUP)
    ang = 2.0 * np.pi * ((n[:, None] * n[None, :]) % FOURIER_GROUP) / FOURIER_GROUP
    return np.stack([np.cos(ang), np.sin(ang)]).astype(np.float32)


def _stage1_matrix(n1):
    n = np.arange(n1)
    ang = 2.0 * np.pi * ((n[:, None] * n[None, :]) % n1) / n1
    c, s = np.cos(ang), np.sin(ang)
    return np.block([[c, -s], [-s, -c]]).astype(np.float32)


def _stage2_matrices(n1):
    s_len = n1 * SEQ_MINOR
    s1p = np.arange(n1)[:, None, None]
    s2p = np.arange(SEQ_MINOR)[None, :, None]
    s2 = np.arange(SEQ_MINOR)[None, None, :]
    ang = 2.0 * np.pi * ((s2 * (s1p + n1 * s2p)) % s_len) / s_len
    scale = 1.0 / math.sqrt(s_len * FOURIER_GROUP)
    return (np.concatenate([np.cos(ang), np.sin(ang)], axis=-1) * scale).astype(np.float32)


def _fourier_weight_kernel(dft_ref, w_ref, ab_ref):
    for g in range(N_FGROUPS):
        w = w_ref[g]
        a = jnp.dot(dft_ref[0], w, preferred_element_type=F32, precision=lax.Precision.HIGHEST)
        b = jnp.dot(dft_ref[1], w, preferred_element_type=F32, precision=lax.Precision.HIGHEST)
        ab_ref[g, :, :FOURIER_GROUP] = a.astype(BF16)
        ab_ref[g, :, FOURIER_GROUP:] = b.astype(BF16)


def _fourier_weights(w_f):
    return pl.pallas_call(
        _fourier_weight_kernel,
        out_shape=jax.ShapeDtypeStruct((N_FGROUPS, FOURIER_GROUP, 2 * FOURIER_GROUP), BF16),
        name="fourier_weights",
    )(jnp.asarray(_dft128_tables()), w_f)


def _inproj_kernel(x_ref, g_ref, w_ref, ab_ref, pq_ref, qt_ref, k_ref, vt_ref, h_ref):
    n = pl.program_id(1)

    @pl.when(n == 0)
    def _():
        x = x_ref[...]
        ms = jnp.mean(x * x, axis=-1, keepdims=True)
        h_ref[...] = (x * lax.rsqrt(ms + EPS) * g_ref[...]).astype(BF16)

    r = jnp.dot(h_ref[...], w_ref[...], preferred_element_type=F32)

    @pl.when(n == 0)
    def _():
        ub = r.astype(BF16)
        for g in range(N_FGROUPS):
            lo, hi = g * FOURIER_GROUP, (g + 1) * FOURIER_GROUP
            pq = jnp.dot(ub[:, lo:hi], ab_ref[g], preferred_element_type=F32)
            pq_ref[0, :, lo:hi] = pq[:, :FOURIER_GROUP]
            pq_ref[1, :, lo:hi] = pq[:, FOURIER_GROUP:]

    @pl.when(n == 1)
    def _():
        qt_ref[0] = (r * (HALF_DIM ** -0.5)).T.astype(BF16)

    @pl.when(n == 2)
    def _():
        k_ref[...] = r.astype(BF16)

    @pl.when(n == 3)
    def _():
        vt_ref[0] = r.T.astype(BF16)


def _inproj(x2d, g, w_in_bf, ab):
    t = x2d.shape[0]
    tm = TOKEN_TILE
    nt = t // tm
    return pl.pallas_call(
        _inproj_kernel,
        grid=(nt, 4),
        in_specs=[
            pl.BlockSpec((tm, D_MODEL), lambda i, n: (i, 0)),
            pl.BlockSpec((1, D_MODEL), lambda i, n: (0, 0)),
            pl.BlockSpec((D_MODEL, 1024), lambda i, n: (0, n)),
            pl.BlockSpec((N_FGROUPS, FOURIER_GROUP, 2 * FOURIER_GROUP), lambda i, n: (0, 0, 0)),
        ],
        out_specs=[
            pl.BlockSpec((2, tm, FOURIER_WIDTH), lambda i, n: (0, i, 0)),
            pl.BlockSpec((1, ATTN_WIDTH, tm), lambda i, n: (i, 0, 0)),
            pl.BlockSpec((tm, ATTN_WIDTH), lambda i, n: (i, 0)),
            pl.BlockSpec((1, ATTN_WIDTH, tm), lambda i, n: (i, 0, 0)),
        ],
        out_shape=[
            jax.ShapeDtypeStruct((2, t, FOURIER_WIDTH), F32),
            jax.ShapeDtypeStruct((nt, ATTN_WIDTH, tm), BF16),
            jax.ShapeDtypeStruct((t, ATTN_WIDTH), BF16),
            jax.ShapeDtypeStruct((nt, ATTN_WIDTH, tm), BF16),
        ],
        scratch_shapes=[pltpu.VMEM((tm, D_MODEL), BF16)],
        compiler_params=_cparams(("arbitrary", "arbitrary")),
        name="inproj",
    )(x2d, g.reshape(1, D_MODEL), w_in_bf, ab)


def _fourier1_kernel(x_ref, d_ref, o_ref, *, n1, rows):
    dl = d_ref[0]
    dr = d_ref[1]
    for r in range(rows):
        p = x_ref[0, :, r, :].astype(BF16)
        q = x_ref[1, :, r, :].astype(BF16)
        y = (jnp.dot(dl, p, preferred_element_type=F32)
             + jnp.dot(dr, q, preferred_element_type=F32))
        o_ref[0, r] = y[:n1]
        o_ref[1, r] = y[n1:]


def _fourier1(pq5, n1):
    b = pq5.shape[1]
    rows = 8 if n1 >= 64 else 64
    d = _stage1_matrix(n1)
    d1 = jnp.asarray(np.stack([d[:, :n1], d[:, n1:]])).astype(BF16)
    return pl.pallas_call(
        functools.partial(_fourier1_kernel, n1=n1, rows=rows),
        grid=(b, SEQ_MINOR // rows),
        in_specs=[
            pl.BlockSpec((2, None, n1, rows, FOURIER_WIDTH), lambda i, j: (0, i, 0, j, 0)),
            pl.BlockSpec((2, 2 * n1, n1), lambda i, j: (0, 0, 0)),
        ],
        out_specs=pl.BlockSpec((None, 2, rows, n1, FOURIER_WIDTH), lambda i, j: (i, 0, j, 0, 0)),
        out_shape=jax.ShapeDtypeStruct((b, 2, SEQ_MINOR, n1, FOURIER_WIDTH), F32),
        compiler_params=_cparams(("arbitrary", "arbitrary")),
        name="fourier_stage1",
    )(pq5, d1)


def _fourier2_kernel(x_ref, m_ref, o_ref, *, rows):
    for i in range(rows):
        xr = x_ref[0, :, i, :].astype(BF16)
        xi = x_ref[1, :, i, :].astype(BF16)
        m = m_ref[i]
        y = (jnp.dot(m[:, :SEQ_MINOR], xr, preferred_element_type=F32)
             + jnp.dot(m[:, SEQ_MINOR:], xi, preferred_element_type=F32))
        o_ref[:, i, :] = y


def _fourier2(y1, n1):
    b = y1.shape[0]
    rows = 8
    m2 = jnp.asarray(_stage2_matrices(n1)).astype(BF16)
    return pl.pallas_call(
        functools.partial(_fourier2_kernel, rows=rows),
        grid=(b, n1 // rows),
        in_specs=[
            pl.BlockSpec((None, 2, SEQ_MINOR, rows, FOURIER_WIDTH), lambda i, j: (i, 0, 0, j, 0)),
            pl.BlockSpec((rows, SEQ_MINOR, 2 * SEQ_MINOR), lambda i, j: (j, 0, 0)),
        ],
        out_specs=pl.BlockSpec((None, SEQ_MINOR, rows, FOURIER_WIDTH), lambda i, j: (i, 0, j, 0)),
        out_shape=jax.ShapeDtypeStruct((b, SEQ_MINOR, n1, FOURIER_WIDTH), F32),
        compiler_params=_cparams(("arbitrary", "arbitrary")),
        name="fourier_stage2",
    )(y1, m2)


def _t5_bucket(rel):
    nb = NUM_BUCKETS // 2
    max_exact = nb // 2
    ret = (rel > 0).astype(jnp.int32) * nb
    n = jnp.abs(rel)
    nf = jnp.maximum(n, 1).astype(jnp.float32)
    large = max_exact + (jnp.log(nf / max_exact) / math.log(MAX_DISTANCE / max_exact)
                         * (nb - max_exact)).astype(jnp.int32)
    large = jnp.minimum(large, nb - 1)
    return ret + jnp.where(n < max_exact, n, large)


def _bias_tiles(rel_bias):
    tk = tq = TOKEN_TILE
    span = (BIAS_TILES // 2) * tk + tk - 1
    rel = jnp.arange(-span, span + 1, dtype=jnp.int32)
    tb = rel_bias.astype(F32)[_t5_bucket(rel)].T
    off = (np.arange(BIAS_TILES)[:, None, None] - BIAS_TILES // 2) * tk
    idx = off + np.arange(tk)[None, :, None] - np.arange(tq)[None, None, :] + span
    return jnp.take(tb, jnp.asarray(idx.astype(np.int32)), axis=1)


def _attn_kernel(lam_ref, qt_ref, k_ref, vt_ref, bias_ref, g_ref, o_ref, m_sc, l_sc, acc_sc, *, nkv, tq, tk):
    qi = pl.program_id(2)
    qt = qt_ref[0]
    row = lax.broadcasted_iota(jnp.int32, (ATTN_HEAD, tq), 0)
    zero = jnp.zeros_like(qt)
    qts = (jnp.where(row < HALF_DIM, qt, zero), jnp.where(row >= HALF_DIM, qt, zero))

    m_sc[...] = jnp.full(m_sc.shape, NEG_BIG, F32)
    l_sc[...] = jnp.zeros(l_sc.shape, F32)
    acc_sc[...] = jnp.zeros(acc_sc.shape, F32)

    def body(j, carry):
        kj = k_ref[pl.ds(pl.multiple_of(j * tk, tk), tk), :]
        vj = vt_ref[j]
        bt = bias_ref[jnp.clip(j - qi, -(BIAS_TILES // 2), BIAS_TILES // 2) + BIAS_TILES // 2]
        for c in range(2):
            s = jnp.dot(kj, qts[c], preferred_element_type=F32) + bt
            m_old = m_sc[c]
            m_new = jnp.maximum(m_old, jnp.max(s, axis=0, keepdims=True))
            alpha = jnp.exp(m_old - m_new)
            p = jnp.exp(s - m_new)
            l_sc[c] = alpha * l_sc[c] + jnp.sum(p, axis=0, keepdims=True)
            acc_sc[c] = alpha * acc_sc[c] + jnp.dot(vj, p.astype(BF16), preferred_element_type=F32)
            m_sc[c] = m_new
        return carry

    lax.fori_loop(0, nkv, body, 0)

    lam = (jnp.exp(jnp.sum(lam_ref[0:1, :] * lam_ref[1:2, :], axis=-1, keepdims=True))
           - jnp.exp(jnp.sum(lam_ref[2:3, :] * lam_ref[3:4, :], axis=-1, keepdims=True))
           + LAM_INIT)
    o = acc_sc[0] / l_sc[0] - lam * (acc_sc[1] / l_sc[1])
    ms = jnp.mean(o * o, axis=0, keepdims=True)
    o = o * lax.rsqrt(ms + EPS) * g_ref[...]
    o = o * (1.0 - LAM_INIT)
    o_ref[...] = o.T.astype(BF16)


def _attention(qt, k, vt, bias, lam_params, subln_g, b, s):
    tq = tk = TOKEN_TILE
    nq = s // tq
    nkv = s // tk
    t = b * s
    return pl.pallas_call(
        functools.partial(_attn_kernel, nkv=nkv, tq=tq, tk=tk),
        grid=(b, N_HEADS, nq),
        in_specs=[
            pl.BlockSpec((4, HALF_DIM), lambda bi, h, qi: (0, 0)),
            pl.BlockSpec((1, ATTN_HEAD, tq), lambda bi, h, qi: (bi * nq + qi, h, 0)),
            pl.BlockSpec((s, ATTN_HEAD), lambda bi, h, qi: (bi, h)),
            pl.BlockSpec((nkv, ATTN_HEAD, tk), lambda bi, h, qi: (bi, h, 0)),
            pl.BlockSpec((None, BIAS_TILES, tk, tq), lambda bi, h, qi: (h, 0, 0, 0)),
            pl.BlockSpec((ATTN_HEAD, 1), lambda bi, h, qi: (0, 0)),
        ],
        out_specs=pl.BlockSpec((tq, ATTN_HEAD), lambda bi, h, qi: (bi * nq + qi, h)),
        out_shape=jax.ShapeDtypeStruct((t, ATTN_WIDTH), BF16),
        scratch_shapes=[
            pltpu.VMEM((2, 1, tq), F32),
            pltpu.VMEM((2, 1, tq), F32),
            pltpu.VMEM((2, ATTN_HEAD, tq), F32),
        ],
        compiler_params=_cparams(("arbitrary", "arbitrary", "arbitrary")),
        name="diff_attention",
    )(lam_params, qt, k, vt, bias, subln_g.reshape(ATTN_HEAD, 1))


def _outproj_kernel(x_ref, yf_ref, ya_ref, w_ref, g_ref, x1_ref, h2_ref):
    y = (jnp.dot(yf_ref[...].astype(BF16), w_ref[:FOURIER_WIDTH, :], preferred_element_type=F32)
         + jnp.dot(ya_ref[...], w_ref[FOURIER_WIDTH:, :], preferred_element_type=F32))
    x1 = x_ref[...] + y
    x1_ref[...] = x1
    ms = jnp.mean(x1 * x1, axis=-1, keepdims=True)
    h2_ref[...] = (x1 * lax.rsqrt(ms + EPS) * g_ref[...]).astype(BF16)


def _outproj(x2d, yf, ya, w_out_bf, g2):
    t = x2d.shape[0]
    tm = TOKEN_TILE
    return pl.pallas_call(
        _outproj_kernel,
        grid=(t // tm,),
        in_specs=[
            pl.BlockSpec((tm, D_MODEL), lambda i: (i, 0)),
            pl.BlockSpec((tm, FOURIER_WIDTH), lambda i: (i, 0)),
            pl.BlockSpec((tm, ATTN_WIDTH), lambda i: (i, 0)),
            pl.BlockSpec((D_MODEL, D_MODEL), lambda i: (0, 0)),
            pl.BlockSpec((1, D_MODEL), lambda i: (0, 0)),
        ],
        out_specs=[
            pl.BlockSpec((tm, D_MODEL), lambda i: (i, 0)),
            pl.BlockSpec((tm, D_MODEL), lambda i: (i, 0)),
        ],
        out_shape=[
            jax.ShapeDtypeStruct((t, D_MODEL), F32),
            jax.ShapeDtypeStruct((t, D_MODEL), BF16),
        ],
        compiler_params=_cparams(("arbitrary",)),
        name="outproj",
    )(x2d, yf, ya, w_out_bf, g2.reshape(1, D_MODEL))


def _ffn_kernel(h_ref, x1_ref, wg_ref, wu_ref, wd_ref, g_ref, o_ref, acc_ref):
    f = pl.program_id(1)

    @pl.when(f == 0)
    def _():
        acc_ref[...] = jnp.zeros(acc_ref.shape, F32)

    h = h_ref[...]
    gate = jnp.dot(h, wg_ref[...], preferred_element_type=F32)
    up = jnp.dot(h, wu_ref[...], preferred_element_type=F32)
    a = (gate * (1.0 / (1.0 + jnp.exp(-gate))) * up).astype(BF16)
    acc_ref[...] += jnp.dot(a, wd_ref[...], preferred_element_type=F32)

    @pl.when(f == pl.num_programs(1) - 1)
    def _():
        x2 = x1_ref[...] + acc_ref[...]
        ms = jnp.mean(x2 * x2, axis=-1, keepdims=True)
        o_ref[...] = x2 * lax.rsqrt(ms + EPS) * g_ref[...]


def _ffn(h2, x1, wg_bf, wu_bf, wd_bf, gf):
    t = h2.shape[0]
    tm = TOKEN_TILE
    tf = FF_TILE
    return pl.pallas_call(
        _ffn_kernel,
        grid=(t // tm, D_FF // tf),
        in_specs=[
            pl.BlockSpec((tm, D_MODEL), lambda i, f: (i, 0)),
            pl.BlockSpec((tm, D_MODEL), lambda i, f: (i, 0)),
            pl.BlockSpec((D_MODEL, tf), lambda i, f: (0, f)),
            pl.BlockSpec((D_MODEL, tf), lambda i, f: (0, f)),
            pl.BlockSpec((tf, D_MODEL), lambda i, f: (f, 0)),
            pl.BlockSpec((1, D_MODEL), lambda i, f: (0, 0)),
        ],
        out_specs=pl.BlockSpec((tm, D_MODEL), lambda i, f: (i, 0)),
        out_shape=jax.ShapeDtypeStruct((t, D_MODEL), F32),
        scratch_shapes=[pltpu.VMEM((tm, D_MODEL), F32)],
        compiler_params=_cparams(("arbitrary", "arbitrary")),
        name="ffn",
    )(h2, x1, wg_bf, wu_bf, wd_bf, gf.reshape(1, D_MODEL))


def _encoder(x, params):
    b, s, _ = x.shape
    t = b * s
    n1 = s // SEQ_MINOR
    x2d = x.reshape(t, D_MODEL)

    pq, qt, k, vt = _inproj(x2d, params["norm1_g"], params["w_in"], params["ab"])
    y1 = _fourier1(pq.reshape(2, b, n1, SEQ_MINOR, FOURIER_WIDTH), n1)
    yf = _fourier2(y1, n1).reshape(t, FOURIER_WIDTH)
    ya = _attention(qt, k, vt, params["bias"], params["lam"], params["subln_g"], b, s)
    x1, h2 = _outproj(x2d, yf, ya, params["w_out"], params["norm2_g"])
    out = _ffn(h2, x1, params["w_gate"], params["w_up"], params["w_down"], params["final_g"])
    return out.reshape(b, s, D_MODEL)


def kernel(x_prompt, x_sample, norm1_g, w_in, w_fourier, lambda_q1, lambda_k1, lambda_q2, lambda_k2,
           subln_g, w_out, norm2_g, w_gate, w_up, w_down, rel_bias, final_g):
    params = {
        "norm1_g": norm1_g[0].astype(F32),
        "w_in": w_in[0].astype(BF16),
        "ab": _fourier_weights(w_fourier[0].astype(F32)),
        "lam": jnp.stack([lambda_q1[0], lambda_k1[0], lambda_q2[0], lambda_k2[0]]).astype(F32),
        "subln_g": subln_g[0].astype(F32),
        "w_out": w_out[0].astype(BF16),
        "norm2_g": norm2_g[0].astype(F32),
        "w_gate": w_gate[0].astype(BF16),
        "w_up": w_up[0].astype(BF16),
        "w_down": w_down[0].astype(BF16),
        "bias": _bias_tiles(rel_bias),
        "final_g": final_g.astype(F32),
    }
    return (_encoder(x_prompt, params), _encoder(x_sample, params))
```

```python
import functools
import math

import numpy as np
import jax
import jax.numpy as jnp
from jax import lax
from jax.experimental import pallas as pl
from jax.experimental.pallas import tpu as pltpu

F32 = jnp.float32
BF16 = jnp.bfloat16

D_MODEL = 2048
FOURIER_WIDTH = 1024
FOURIER_GROUP = 128
N_FGROUPS = 8
ATTN_WIDTH = 1024
ATTN_HEAD = 128
N_HEADS = 8
HALF_DIM = 64
D_FF = 5632
NUM_BUCKETS = 32
MAX_DISTANCE = 128
EPS = 1e-6
LAM_INIT = 0.8 - 0.6 * math.exp(-0.3 * 0)

SEQ_MINOR = 128
TOKEN_TILE = 512
FF_TILE = 512
ATTN_Q_TILE = 1024
ATTN_LANE_CHUNK = 256
LOG2E = math.log2(math.e)
NEG_BIG = -1e30
VMEM_LIMIT = 56 * 1024 * 1024


def _cparams(sem):
    return pltpu.CompilerParams(dimension_semantics=sem, vmem_limit_bytes=VMEM_LIMIT)


def _dft128_tables():
    n = np.arange(FOURIER_GROUP)
    ang = 2.0 * np.pi * ((n[:, None] * n[None, :]) % FOURIER_GROUP) / FOURIER_GROUP
    return np.stack([np.cos(ang), np.sin(ang)]).astype(np.float32)


def _stage1_matrix(n1):
    n = np.arange(n1)
    ang = 2.0 * np.pi * ((n[:, None] * n[None, :]) % n1) / n1
    c, s = np.cos(ang), np.sin(ang)
    return np.block([[c, -s], [-s, -c]]).astype(np.float32)


def _stage2_matrices(n1):
    s_len = n1 * SEQ_MINOR
    s1p = np.arange(n1)[:, None, None]
    s2p = np.arange(SEQ_MINOR)[None, :, None]
    s2 = np.arange(SEQ_MINOR)[None, None, :]
    ang = 2.0 * np.pi * ((s2 * (s1p + n1 * s2p)) % s_len) / s_len
    scale = 1.0 / math.sqrt(s_len * FOURIER_GROUP)
    return (np.concatenate([np.cos(ang), np.sin(ang)], axis=-1) * scale).astype(np.float32)


def _fourier_weight_kernel(dft_ref, w_ref, ab_ref):
    for g in range(N_FGROUPS):
        w = w_ref[g]
        a = jnp.dot(dft_ref[0], w, preferred_element_type=F32, precision=lax.Precision.HIGHEST)
        b = jnp.dot(dft_ref[1], w, preferred_element_type=F32, precision=lax.Precision.HIGHEST)
        ab_ref[g, :, :FOURIER_GROUP] = a.astype(BF16)
        ab_ref[g, :, FOURIER_GROUP:] = b.astype(BF16)


def _fourier_weights(w_f):
    return pl.pallas_call(
        _fourier_weight_kernel,
        out_shape=jax.ShapeDtypeStruct((N_FGROUPS, FOURIER_GROUP, 2 * FOURIER_GROUP), BF16),
        name="fourier_weights",
    )(jnp.asarray(_dft128_tables()), w_f)


def _inproj_kernel(x_ref, g_ref, w_ref, ab_ref, pq_ref, qt_ref, k_ref, vt_ref, h_ref):
    n = pl.program_id(1)

    @pl.when(n == 0)
    def _():
        x = x_ref[...]
        ms = jnp.mean(x * x, axis=-1, keepdims=True)
        h_ref[...] = (x * lax.rsqrt(ms + EPS) * g_ref[...]).astype(BF16)

    r = jnp.dot(h_ref[...], w_ref[...], preferred_element_type=F32)

    @pl.when(n == 0)
    def _():
        ub = r.astype(BF16)
        for g in range(N_FGROUPS):
            lo, hi = g * FOURIER_GROUP, (g + 1) * FOURIER_GROUP
            pq = jnp.dot(ub[:, lo:hi], ab_ref[g], preferred_element_type=F32)
            pq_ref[0, :, lo:hi] = pq[:, :FOURIER_GROUP]
            pq_ref[1, :, lo:hi] = pq[:, FOURIER_GROUP:]

    @pl.when(n == 1)
    def _():
        qt_ref[0] = (r * (HALF_DIM ** -0.5 * LOG2E)).T.astype(BF16)

    @pl.when(n == 2)
    def _():
        k_ref[...] = r.astype(BF16)

    @pl.when(n == 3)
    def _():
        vt_ref[0] = r.T.astype(BF16)


def _inproj(x2d, g, w_in_bf, ab):
    t = x2d.shape[0]
    tm = TOKEN_TILE
    nt = t // tm
    return pl.pallas_call(
        _inproj_kernel,
        grid=(nt, 4),
        in_specs=[
            pl.BlockSpec((tm, D_MODEL), lambda i, n: (i, 0)),
            pl.BlockSpec((1, D_MODEL), lambda i, n: (0, 0)),
            pl.BlockSpec((D_MODEL, 1024), lambda i, n: (0, n)),
            pl.BlockSpec((N_FGROUPS, FOURIER_GROUP, 2 * FOURIER_GROUP), lambda i, n: (0, 0, 0)),
        ],
        out_specs=[
            pl.BlockSpec((2, tm, FOURIER_WIDTH), lambda i, n: (0, i, 0)),
            pl.BlockSpec((1, ATTN_WIDTH, tm), lambda i, n: (i, 0, 0)),
            pl.BlockSpec((tm, ATTN_WIDTH), lambda i, n: (i, 0)),
            pl.BlockSpec((1, ATTN_WIDTH, tm), lambda i, n: (i, 0, 0)),
        ],
        out_shape=[
            jax.ShapeDtypeStruct((2, t, FOURIER_WIDTH), F32),
            jax.ShapeDtypeStruct((nt, ATTN_WIDTH, tm), BF16),
            jax.ShapeDtypeStruct((t, ATTN_WIDTH), BF16),
            jax.ShapeDtypeStruct((nt, ATTN_WIDTH, tm), BF16),
        ],
        scratch_shapes=[pltpu.VMEM((tm, D_MODEL), BF16)],
        compiler_params=_cparams(("arbitrary", "arbitrary")),
        name="inproj",
    )(x2d, g.reshape(1, D_MODEL), w_in_bf, ab)


def _fourier1_kernel(x_ref, d_ref, o_ref, *, n1, rows):
    dl = d_ref[0]
    dr = d_ref[1]
    for r in range(rows):
        p = x_ref[0, :, r, :].astype(BF16)
        q = x_ref[1, :, r, :].astype(BF16)
        y = (jnp.dot(dl, p, preferred_element_type=F32)
             + jnp.dot(dr, q, preferred_element_type=F32))
        o_ref[0, r] = y[:n1]
        o_ref[1, r] = y[n1:]


def _fourier1(pq5, n1):
    b = pq5.shape[1]
    rows = 8 if n1 >= 64 else 64
    d = _stage1_matrix(n1)
    d1 = jnp.asarray(np.stack([d[:, :n1], d[:, n1:]])).astype(BF16)
    return pl.pallas_call(
        functools.partial(_fourier1_kernel, n1=n1, rows=rows),
        grid=(b, SEQ_MINOR // rows),
        in_specs=[
            pl.BlockSpec((2, None, n1, rows, FOURIER_WIDTH), lambda i, j: (0, i, 0, j, 0)),
            pl.BlockSpec((2, 2 * n1, n1), lambda i, j: (0, 0, 0)),
        ],
        out_specs=pl.BlockSpec((None, 2, rows, n1, FOURIER_WIDTH), lambda i, j: (i, 0, j, 0, 0)),
        out_shape=jax.ShapeDtypeStruct((b, 2, SEQ_MINOR, n1, FOURIER_WIDTH), F32),
        compiler_params=_cparams(("arbitrary", "arbitrary")),
        name="fourier_stage1",
    )(pq5, d1)


def _fourier2_kernel(x_ref, m_ref, o_ref, *, rows):
    for i in range(rows):
        xr = x_ref[0, :, i, :].astype(BF16)
        xi = x_ref[1, :, i, :].astype(BF16)
        m = m_ref[i]
        y = (jnp.dot(m[:, :SEQ_MINOR], xr, preferred_element_type=F32)
             + jnp.dot(m[:, SEQ_MINOR:], xi, preferred_element_type=F32))
        o_ref[:, i, :] = y


def _fourier2(y1, n1):
    b = y1.shape[0]
    rows = 8
    m2 = jnp.asarray(_stage2_matrices(n1)).astype(BF16)
    return pl.pallas_call(
        functools.partial(_fourier2_kernel, rows=rows),
        grid=(b, n1 // rows),
        in_specs=[
            pl.BlockSpec((None, 2, SEQ_MINOR, rows, FOURIER_WIDTH), lambda i, j: (i, 0, 0, j, 0)),
            pl.BlockSpec((rows, SEQ_MINOR, 2 * SEQ_MINOR), lambda i, j: (j, 0, 0)),
        ],
        out_specs=pl.BlockSpec((None, SEQ_MINOR, rows, FOURIER_WIDTH), lambda i, j: (i, 0, j, 0)),
        out_shape=jax.ShapeDtypeStruct((b, SEQ_MINOR, n1, FOURIER_WIDTH), F32),
        compiler_params=_cparams(("arbitrary", "arbitrary")),
        name="fourier_stage2",
    )(y1, m2)


def _t5_bucket(rel):
    nb = NUM_BUCKETS // 2
    max_exact = nb // 2
    ret = (rel > 0).astype(jnp.int32) * nb
    n = jnp.abs(rel)
    nf = jnp.maximum(n, 1).astype(jnp.float32)
    large = max_exact + (jnp.log(nf / max_exact) / math.log(MAX_DISTANCE / max_exact)
                         * (nb - max_exact)).astype(jnp.int32)
    large = jnp.minimum(large, nb - 1)
    return ret + jnp.where(n < max_exact, n, large)


def _bias_tables(rel_bias):
    tq, tk = ATTN_Q_TILE, TOKEN_TILE
    r = tq // tk
    wd = tq + tk
    mp = np.arange(wd)
    m = np.where(mp < tq, mp, mp - wd)
    d = (np.arange(r + 2)[:, None] - 1) * tk - m[None, :]
    tbl = rel_bias.astype(F32) * LOG2E
    w = tbl[_t5_bucket(jnp.asarray(d, jnp.int32))]
    wtab = jnp.transpose(w, (2, 0, 1))[:, :, None, :]
    far = tbl[_t5_bucket(jnp.asarray([-2 * tk, 2 * tk], jnp.int32))]
    cfar = jnp.broadcast_to(far.T[:, :, None, None], (N_HEADS, 2, 1, tq))
    return wtab, cfar


def _attn_kernel(lam_ref, qt_ref, k_ref, vt_ref, w_ref, cfar_ref, g_ref, o_ref, m_sc, l_sc, acc_sc, bias_sc,
                 *, nkv, tq, tk):
    qi = pl.program_id(2)
    r = tq // tk
    w = ATTN_LANE_CHUNK
    chunks = tq // w

    @pl.when(qi == 0)
    def _():
        for t in range(r + 2):
            x = jnp.broadcast_to(w_ref[t], (tk, tq + tk))
            bias_sc[t] = pltpu.roll(x, 0, 1, stride=1, stride_axis=0)[:, :tq]

    row = lax.broadcasted_iota(jnp.int32, (ATTN_HEAD, w), 0)
    qts = []
    for lc in range(chunks):
        q = qt_ref[(lc * w) // tk][:, (lc * w) % tk:(lc * w) % tk + w]
        zero = jnp.zeros_like(q)
        qts.append((jnp.where(row < HALF_DIM, q, zero), jnp.where(row >= HALF_DIM, q, zero)))

    m_sc[...] = jnp.full(m_sc.shape, NEG_BIG, F32)
    l_sc[...] = jnp.zeros(l_sc.shape, F32)
    acc_sc[...] = jnp.zeros(acc_sc.shape, F32)

    def block(j, bias_view, c_row):
        kj = k_ref[pl.ds(pl.multiple_of(j * tk, tk), tk), :]
        vj = vt_ref[j]
        ss = [[jnp.dot(kj, qts[lc][c], preferred_element_type=F32) for lc in range(chunks)] for c in range(2)]
        for c in range(2):
            for lc in range(chunks):
                sl = slice(lc * w, (lc + 1) * w)
                s = ss[c][lc]
                m_old = m_sc[c, :, sl]
                if bias_view is not None:
                    s = jnp.maximum(s, NEG_BIG) + bias_view[:, sl]
                    m_new = jnp.maximum(m_old, jnp.max(s, axis=0, keepdims=True))
                    p = jnp.exp2(s - m_new)
                else:
                    cr = c_row[:, sl]
                    m_new = jnp.maximum(m_old, jnp.max(s, axis=0, keepdims=True) + cr)
                    p = jnp.exp2(s - (m_new - cr))
                alpha = jnp.exp2(m_old - m_new)
                l_sc[c, :, sl] = alpha * l_sc[c, :, sl] + jnp.sum(p, axis=0, keepdims=True)
                acc_sc[c, :, sl] = alpha * acc_sc[c, :, sl] + jnp.dot(vj, p.astype(BF16), preferred_element_type=F32)
                m_sc[c, :, sl] = m_new

    def far(c_row):
        def body(j, carry):
            block(j, None, c_row)
            return carry
        return body

    def near(j, carry):
        block(j, bias_sc.at[j - (qi * r - 1)], None)
        return carry

    lo = jnp.maximum(qi * r - 1, 0)
    hi = jnp.minimum((qi + 1) * r + 1, nkv)
    lax.fori_loop(0, lo, far(cfar_ref[0]), 0)
    lax.fori_loop(lo, hi, near, 0)
    lax.fori_loop(hi, nkv, far(cfar_ref[1]), 0)

    lam = (jnp.exp(jnp.sum(lam_ref[0:1, :] * lam_ref[1:2, :], axis=-1, keepdims=True))
           - jnp.exp(jnp.sum(lam_ref[2:3, :] * lam_ref[3:4, :], axis=-1, keepdims=True))
           + LAM_INIT)
    o = acc_sc[0] / l_sc[0] - lam * (acc_sc[1] / l_sc[1])
    ms = jnp.mean(o * o, axis=0, keepdims=True)
    o = o * lax.rsqrt(ms + EPS) * g_ref[...]
    o = o * (1.0 - LAM_INIT)
    o_ref[...] = o.T.astype(BF16)


def _attention(qt, k, vt, wtab, cfar, lam_params, subln_g, b, s):
    tq, tk = ATTN_Q_TILE, TOKEN_TILE
    r = tq // tk
    nq = s // tq
    nkv = s // tk
    t = b * s
    return pl.pallas_call(
        functools.partial(_attn_kernel, nkv=nkv, tq=tq, tk=tk),
        grid=(b, N_HEADS, nq),
        in_specs=[
            pl.BlockSpec((4, HALF_DIM), lambda bi, h, qi: (0, 0)),
            pl.BlockSpec((r, ATTN_HEAD, tk), lambda bi, h, qi: (bi * nq + qi, h, 0)),
            pl.BlockSpec((s, ATTN_HEAD), lambda bi, h, qi: (bi, h)),
            pl.BlockSpec((nkv, ATTN_HEAD, tk), lambda bi, h, qi: (bi, h, 0)),
            pl.BlockSpec((None, r + 2, 1, tq + tk), lambda bi, h, qi: (h, 0, 0, 0)),
            pl.BlockSpec((None, 2, 1, tq), lambda bi, h, qi: (h, 0, 0, 0)),
            pl.BlockSpec((ATTN_HEAD, 1), lambda bi, h, qi: (0, 0)),
        ],
        out_specs=pl.BlockSpec((tq, ATTN_HEAD), lambda bi, h, qi: (bi * nq + qi, h)),
        out_shape=jax.ShapeDtypeStruct((t, ATTN_WIDTH), BF16),
        scratch_shapes=[
            pltpu.VMEM((2, 1, tq), F32),
            pltpu.VMEM((2, 1, tq), F32),
            pltpu.VMEM((2, ATTN_HEAD, tq), F32),
            pltpu.VMEM((r + 2, tk, tq), F32),
        ],
        compiler_params=_cparams(("arbitrary", "arbitrary", "arbitrary")),
        name="diff_attention",
    )(lam_params, qt, k, vt, wtab, cfar, subln_g.reshape(ATTN_HEAD, 1))


def _outproj_kernel(x_ref, yf_ref, ya_ref, w_ref, g_ref, x1_ref, h2_ref):
    y = (jnp.dot(yf_ref[...].astype(BF16), w_ref[:FOURIER_WIDTH, :], preferred_element_type=F32)
         + jnp.dot(ya_ref[...], w_ref[FOURIER_WIDTH:, :], preferred_element_type=F32))
    x1 = x_ref[...] + y
    x1_ref[...] = x1
    ms = jnp.mean(x1 * x1, axis=-1, keepdims=True)
    h2_ref[...] = (x1 * lax.rsqrt(ms + EPS) * g_ref[...]).astype(BF16)


def _outproj(x2d, yf, ya, w_out_bf, g2):
    t = x2d.shape[0]
    tm = TOKEN_TILE
    return pl.pallas_call(
        _outproj_kernel,
        grid=(t // tm,),
        in_specs=[
            pl.BlockSpec((tm, D_MODEL), lambda i: (i, 0)),
            pl.BlockSpec((tm, FOURIER_WIDTH), lambda i: (i, 0)),
            pl.BlockSpec((tm, ATTN_WIDTH), lambda i: (i, 0)),
            pl.BlockSpec((D_MODEL, D_MODEL), lambda i: (0, 0)),
            pl.BlockSpec((1, D_MODEL), lambda i: (0, 0)),
        ],
        out_specs=[
            pl.BlockSpec((tm, D_MODEL), lambda i: (i, 0)),
            pl.BlockSpec((tm, D_MODEL), lambda i: (i, 0)),
        ],
        out_shape=[
            jax.ShapeDtypeStruct((t, D_MODEL), F32),
            jax.ShapeDtypeStruct((t, D_MODEL), BF16),
        ],
        compiler_params=_cparams(("arbitrary",)),
        name="outproj",
    )(x2d, yf, ya, w_out_bf, g2.reshape(1, D_MODEL))


def _ffn_kernel(h_ref, x1_ref, wg_ref, wu_ref, wd_ref, g_ref, o_ref, acc_ref):
    f = pl.program_id(1)

    @pl.when(f == 0)
    def _():
        acc_ref[...] = jnp.zeros(acc_ref.shape, F32)

    h = h_ref[...]
    gate = jnp.dot(h, wg_ref[...], preferred_element_type=F32)
    up = jnp.dot(h, wu_ref[...], preferred_element_type=F32)
    a = (gate * (1.0 / (1.0 + jnp.exp(-gate))) * up).astype(BF16)
    acc_ref[...] += jnp.dot(a, wd_ref[...], preferred_element_type=F32)

    @pl.when(f == pl.num_programs(1) - 1)
    def _():
        x2 = x1_ref[...] + acc_ref[...]
        ms = jnp.mean(x2 * x2, axis=-1, keepdims=True)
        o_ref[...] = x2 * lax.rsqrt(ms + EPS) * g_ref[...]


def _ffn(h2, x1, wg_bf, wu_bf, wd_bf, gf):
    t = h2.shape[0]
    tm = TOKEN_TILE
    tf = FF_TILE
    return pl.pallas_call(
        _ffn_kernel,
        grid=(t // tm, D_FF // tf),
        in_specs=[
            pl.BlockSpec((tm, D_MODEL), lambda i, f: (i, 0)),
            pl.BlockSpec((tm, D_MODEL), lambda i, f: (i, 0)),
            pl.BlockSpec((D_MODEL, tf), lambda i, f: (0, f)),
            pl.BlockSpec((D_MODEL, tf), lambda i, f: (0, f)),
            pl.BlockSpec((tf, D_MODEL), lambda i, f: (f, 0)),
            pl.BlockSpec((1, D_MODEL), lambda i, f: (0, 0)),
        ],
        out_specs=pl.BlockSpec((tm, D_MODEL), lambda i, f: (i, 0)),
        out_shape=jax.ShapeDtypeStruct((t, D_MODEL), F32),
        scratch_shapes=[pltpu.VMEM((tm, D_MODEL), F32)],
        compiler_params=_cparams(("arbitrary", "arbitrary")),
        name="ffn",
    )(h2, x1, wg_bf, wu_bf, wd_bf, gf.reshape(1, D_MODEL))


def _encoder(x, params):
    b, s, _ = x.shape
    t = b * s
    n1 = s // SEQ_MINOR
    x2d = x.reshape(t, D_MODEL)

    pq, qt, k, vt = _inproj(x2d, params["norm1_g"], params["w_in"], params["ab"])
    y1 = _fourier1(pq.reshape(2, b, n1, SEQ_MINOR, FOURIER_WIDTH), n1)
    yf = _fourier2(y1, n1).reshape(t, FOURIER_WIDTH)
    ya = _attention(qt, k, vt, params["wtab"], params["cfar"], params["lam"], params["subln_g"], b, s)
    x1, h2 = _outproj(x2d, yf, ya, params["w_out"], params["norm2_g"])
    out = _ffn(h2, x1, params["w_gate"], params["w_up"], params["w_down"], params["final_g"])
    return out.reshape(b, s, D_MODEL)


def kernel(x_prompt, x_sample, norm1_g, w_in, w_fourier, lambda_q1, lambda_k1, lambda_q2, lambda_k2,
           subln_g, w_out, norm2_g, w_gate, w_up, w_down, rel_bias, final_g):
    wtab, cfar = _bias_tables(rel_bias)
    params = {
        "norm1_g": norm1_g[0].astype(F32),
        "w_in": w_in[0].astype(BF16),
        "ab": _fourier_weights(w_fourier[0].astype(F32)),
        "lam": jnp.stack([lambda_q1[0], lambda_k1[0], lambda_q2[0], lambda_k2[0]]).astype(F32),
        "subln_g": subln_g[0].astype(F32),
        "w_out": w_out[0].astype(BF16),
        "norm2_g": norm2_g[0].astype(F32),
        "w_gate": w_gate[0].astype(BF16),
        "w_up": w_up[0].astype(BF16),
        "w_down": w_down[0].astype(BF16),
        "wtab": wtab,
        "cfar": cfar,
        "final_g": final_g.astype(F32),
    }
    return (_encoder(x_prompt, params), _encoder(x_sample, params))
```

```python
import functools
import math

import numpy as np
import jax
import jax.numpy as jnp
from jax import lax
from jax.experimental import pallas as pl
from jax.experimental.pallas import tpu as pltpu

F32 = jnp.float32
BF16 = jnp.bfloat16

D_MODEL = 2048
FOURIER_WIDTH = 1024
FOURIER_GROUP = 128
N_FGROUPS = 8
ATTN_WIDTH = 1024
ATTN_HEAD = 128
N_HEADS = 8
HALF_DIM = 64
D_FF = 5632
NUM_BUCKETS = 32
MAX_DISTANCE = 128
EPS = 1e-6
LAM_INIT = 0.8 - 0.6 * math.exp(-0.3 * 0)

SEQ_MINOR = 128
TOKEN_TILE = 512
FF_TILE = 512
ATTN_Q_TILE = 1024
ATTN_KV_TILE = 1024
ATTN_EXP_GUARD = 64.0
ATTN_LANE_CHUNK = 256
LOG2E = math.log2(math.e)
NEG_BIG = -1e30
VMEM_LIMIT = 56 * 1024 * 1024


def _cparams(sem):
    return pltpu.CompilerParams(dimension_semantics=sem, vmem_limit_bytes=VMEM_LIMIT)


def _dft128_tables():
    n = np.arange(FOURIER_GROUP)
    ang = 2.0 * np.pi * ((n[:, None] * n[None, :]) % FOURIER_GROUP) / FOURIER_GROUP
    return np.stack([np.cos(ang), np.sin(ang)]).astype(np.float32)


def _stage1_matrix(n1):
    n = np.arange(n1)
    ang = 2.0 * np.pi * ((n[:, None] * n[None, :]) % n1) / n1
    c, s = np.cos(ang), np.sin(ang)
    return np.block([[c, -s], [-s, -c]]).astype(np.float32)


def _stage2_matrices(n1):
    s_len = n1 * SEQ_MINOR
    s1p = np.arange(n1)[:, None, None]
    s2p = np.arange(SEQ_MINOR)[None, :, None]
    s2 = np.arange(SEQ_MINOR)[None, None, :]
    ang = 2.0 * np.pi * ((s2 * (s1p + n1 * s2p)) % s_len) / s_len
    scale = 1.0 / math.sqrt(s_len * FOURIER_GROUP)
    return (np.concatenate([np.cos(ang), np.sin(ang)], axis=-1) * scale).astype(np.float32)


def _fourier_weight_kernel(dft_ref, w_ref, ab_ref):
    for g in range(N_FGROUPS):
        w = w_ref[g]
        a = jnp.dot(dft_ref[0], w, preferred_element_type=F32, precision=lax.Precision.HIGHEST)
        b = jnp.dot(dft_ref[1], w, preferred_element_type=F32, precision=lax.Precision.HIGHEST)
        ab_ref[g, :, :FOURIER_GROUP] = a.astype(BF16)
        ab_ref[g, :, FOURIER_GROUP:] = b.astype(BF16)


def _fourier_weights(w_f):
    return pl.pallas_call(
        _fourier_weight_kernel,
        out_shape=jax.ShapeDtypeStruct((N_FGROUPS, FOURIER_GROUP, 2 * FOURIER_GROUP), BF16),
        name="fourier_weights",
    )(jnp.asarray(_dft128_tables()), w_f)


def _inproj_kernel(x_ref, g_ref, w_ref, ab_ref, pq_ref, qt_ref, k_ref, vt_ref, h_ref):
    n = pl.program_id(1)

    @pl.when(n == 0)
    def _():
        x = x_ref[...]
        ms = jnp.mean(x * x, axis=-1, keepdims=True)
        h_ref[...] = (x * lax.rsqrt(ms + EPS) * g_ref[...]).astype(BF16)

    r = jnp.dot(h_ref[...], w_ref[...], preferred_element_type=F32)

    @pl.when(n == 0)
    def _():
        ub = r.astype(BF16)
        for g in range(N_FGROUPS):
            lo, hi = g * FOURIER_GROUP, (g + 1) * FOURIER_GROUP
            pq = jnp.dot(ub[:, lo:hi], ab_ref[g], preferred_element_type=F32)
            pq_ref[0, :, lo:hi] = pq[:, :FOURIER_GROUP]
            pq_ref[1, :, lo:hi] = pq[:, FOURIER_GROUP:]

    @pl.when(n == 1)
    def _():
        qt_ref[0] = (r * (HALF_DIM ** -0.5 * LOG2E)).T.astype(BF16)

    @pl.when(n == 2)
    def _():
        k_ref[...] = r.astype(BF16)

    @pl.when(n == 3)
    def _():
        vt_ref[0] = r.T.astype(BF16)


def _inproj(x2d, g, w_in_bf, ab):
    t = x2d.shape[0]
    tm = TOKEN_TILE
    nt = t // tm
    return pl.pallas_call(
        _inproj_kernel,
        grid=(nt, 4),
        in_specs=[
            pl.BlockSpec((tm, D_MODEL), lambda i, n: (i, 0)),
            pl.BlockSpec((1, D_MODEL), lambda i, n: (0, 0)),
            pl.BlockSpec((D_MODEL, 1024), lambda i, n: (0, n)),
            pl.BlockSpec((N_FGROUPS, FOURIER_GROUP, 2 * FOURIER_GROUP), lambda i, n: (0, 0, 0)),
        ],
        out_specs=[
            pl.BlockSpec((2, tm, FOURIER_WIDTH), lambda i, n: (0, i, 0)),
            pl.BlockSpec((1, ATTN_WIDTH, tm), lambda i, n: (i, 0, 0)),
            pl.BlockSpec((tm, ATTN_WIDTH), lambda i, n: (i, 0)),
            pl.BlockSpec((1, ATTN_WIDTH, tm), lambda i, n: (i, 0, 0)),
        ],
        out_shape=[
            jax.ShapeDtypeStruct((2, t, FOURIER_WIDTH), F32),
            jax.ShapeDtypeStruct((nt, ATTN_WIDTH, tm), BF16),
            jax.ShapeDtypeStruct((t, ATTN_WIDTH), BF16),
            jax.ShapeDtypeStruct((nt, ATTN_WIDTH, tm), BF16),
        ],
        scratch_shapes=[pltpu.VMEM((tm, D_MODEL), BF16)],
        compiler_params=_cparams(("arbitrary", "arbitrary")),
        name="inproj",
    )(x2d, g.reshape(1, D_MODEL), w_in_bf, ab)


def _fourier1_kernel(x_ref, d_ref, o_ref, *, n1, rows):
    dl = d_ref[0]
    dr = d_ref[1]
    for r in range(rows):
        p = x_ref[0, :, r, :].astype(BF16)
        q = x_ref[1, :, r, :].astype(BF16)
        y = (jnp.dot(dl, p, preferred_element_type=F32)
             + jnp.dot(dr, q, preferred_element_type=F32))
        o_ref[0, r] = y[:n1]
        o_ref[1, r] = y[n1:]


def _fourier1(pq5, n1):
    b = pq5.shape[1]
    rows = 8 if n1 >= 64 else 64
    d = _stage1_matrix(n1)
    d1 = jnp.asarray(np.stack([d[:, :n1], d[:, n1:]])).astype(BF16)
    return pl.pallas_call(
        functools.partial(_fourier1_kernel, n1=n1, rows=rows),
        grid=(b, SEQ_MINOR // rows),
        in_specs=[
            pl.BlockSpec((2, None, n1, rows, FOURIER_WIDTH), lambda i, j: (0, i, 0, j, 0)),
            pl.BlockSpec((2, 2 * n1, n1), lambda i, j: (0, 0, 0)),
        ],
        out_specs=pl.BlockSpec((None, 2, rows, n1, FOURIER_WIDTH), lambda i, j: (i, 0, j, 0, 0)),
        out_shape=jax.ShapeDtypeStruct((b, 2, SEQ_MINOR, n1, FOURIER_WIDTH), F32),
        compiler_params=_cparams(("arbitrary", "arbitrary")),
        name="fourier_stage1",
    )(pq5, d1)


def _fourier2_kernel(x_ref, m_ref, o_ref, *, rows):
    for i in range(rows):
        xr = x_ref[0, :, i, :].astype(BF16)
        xi = x_ref[1, :, i, :].astype(BF16)
        m = m_ref[i]
        y = (jnp.dot(m[:, :SEQ_MINOR], xr, preferred_element_type=F32)
             + jnp.dot(m[:, SEQ_MINOR:], xi, preferred_element_type=F32))
        o_ref[:, i, :] = y


def _fourier2(y1, n1):
    b = y1.shape[0]
    rows = 8
    m2 = jnp.asarray(_stage2_matrices(n1)).astype(BF16)
    return pl.pallas_call(
        functools.partial(_fourier2_kernel, rows=rows),
        grid=(b, n1 // rows),
        in_specs=[
            pl.BlockSpec((None, 2, SEQ_MINOR, rows, FOURIER_WIDTH), lambda i, j: (i, 0, 0, j, 0)),
            pl.BlockSpec((rows, SEQ_MINOR, 2 * SEQ_MINOR), lambda i, j: (j, 0, 0)),
        ],
        out_specs=pl.BlockSpec((None, SEQ_MINOR, rows, FOURIER_WIDTH), lambda i, j: (i, 0, j, 0)),
        out_shape=jax.ShapeDtypeStruct((b, SEQ_MINOR, n1, FOURIER_WIDTH), F32),
        compiler_params=_cparams(("arbitrary", "arbitrary")),
        name="fourier_stage2",
    )(y1, m2)


def _t5_bucket(rel):
    nb = NUM_BUCKETS // 2
    max_exact = nb // 2
    ret = (rel > 0).astype(jnp.int32) * nb
    n = jnp.abs(rel)
    nf = jnp.maximum(n, 1).astype(jnp.float32)
    large = max_exact + (jnp.log(nf / max_exact) / math.log(MAX_DISTANCE / max_exact)
                         * (nb - max_exact)).astype(jnp.int32)
    large = jnp.minimum(large, nb - 1)
    return ret + jnp.where(n < max_exact, n, large)


def _bias_tables(rel_bias):
    tq, tk = ATTN_Q_TILE, ATTN_KV_TILE
    r = tq // tk
    wd = tq + tk
    mp = np.arange(wd)
    m = np.where(mp < tq, mp, mp - wd)
    d = (np.arange(r + 2)[:, None] - 1) * tk - m[None, :]
    tbl = rel_bias.astype(F32) * LOG2E
    w = tbl[_t5_bucket(jnp.asarray(d, jnp.int32))]
    wtab = jnp.transpose(w, (2, 0, 1))[:, :, None, :]
    far = tbl[_t5_bucket(jnp.asarray([-2 * tk, 2 * tk], jnp.int32))]
    cfar = jnp.broadcast_to(far.T[:, :, None, None], (N_HEADS, 2, 1, tq))
    return wtab, cfar


def _attn_kernel(lam_ref, qt_ref, k_ref, vt_ref, w_ref, cfar_ref, g_ref, o_ref, m_sc, l_sc, acc_sc, bias_sc, ex_sc,
                 *, nkv, tq, tk):
    qi = pl.program_id(2)
    r = tq // tk
    w = ATTN_LANE_CHUNK
    chunks = tq // w
    pieces = tk // TOKEN_TILE

    @pl.when(qi == 0)
    def _():
        for t in range(r + 2):
            x = jnp.broadcast_to(w_ref[t], (tk, tq + tk))
            bias_sc[t] = pltpu.roll(x, 0, 1, stride=1, stride_axis=0)[:, :tq]

    row = lax.broadcasted_iota(jnp.int32, (ATTN_HEAD, w), 0)
    qts = []
    for lc in range(chunks):
        q = qt_ref[(lc * w) // TOKEN_TILE][:, (lc * w) % TOKEN_TILE:(lc * w) % TOKEN_TILE + w]
        zero = jnp.zeros_like(q)
        qts.append((jnp.where(row < HALF_DIM, q, zero), jnp.where(row >= HALF_DIM, q, zero)))

    lo = jnp.maximum(qi * r - 1, 0)
    hi = jnp.minimum((qi + 1) * r + 1, nkv)

    def logits(j, bias_view, c, lc):
        kj = k_ref[pl.ds(pl.multiple_of(j * tk, tk), tk), :]
        s = jnp.dot(kj, qts[lc][c], preferred_element_type=F32)
        if bias_view is not None:
            s = jnp.maximum(s, NEG_BIG) + bias_view[:, lc * w:(lc + 1) * w]
        return s

    def pv(j, p):
        pb = p.astype(BF16)
        out = jnp.dot(vt_ref[j * pieces], pb[:TOKEN_TILE], preferred_element_type=F32)
        for i in range(1, pieces):
            out += jnp.dot(vt_ref[j * pieces + i], pb[i * TOKEN_TILE:(i + 1) * TOKEN_TILE],
                           preferred_element_type=F32)
        return out

    def safe_block(j, bias_view, c_row):
        ss = [[logits(j, bias_view, c, lc) for lc in range(chunks)] for c in range(2)]
        for c in range(2):
            for lc in range(chunks):
                sl = slice(lc * w, (lc + 1) * w)
                s = ss[c][lc]
                m_old = m_sc[c, :, sl]
                if bias_view is not None:
                    m_new = jnp.maximum(m_old, jnp.max(s, axis=0, keepdims=True))
                    p = jnp.exp2(s - m_new)
                else:
                    cr = c_row[:, sl]
                    m_new = jnp.maximum(m_old, jnp.max(s, axis=0, keepdims=True) + cr)
                    p = jnp.exp2(s - (m_new - cr))
                alpha = jnp.exp2(m_old - m_new)
                l_sc[c, :, sl] = alpha * l_sc[c, :, sl] + jnp.sum(p, axis=0, keepdims=True)
                acc_sc[c, :, sl] = alpha * acc_sc[c, :, sl] + pv(j, p)
                m_sc[c, :, sl] = m_new

    def fast_block(j, bias_view, c_row):
        ss = [[logits(j, bias_view, c, lc) for lc in range(chunks)] for c in range(2)]
        for c in range(2):
            for lc in range(chunks):
                sl = slice(lc * w, (lc + 1) * w)
                s = ss[c][lc]
                m_old = m_sc[c, :, sl]
                if bias_view is not None:
                    p = jnp.exp2(s - m_old)
                    bm = jnp.max(s, axis=0, keepdims=True)
                else:
                    cr = c_row[:, sl]
                    p = jnp.exp2(s - (m_old - cr))
                    bm = jnp.max(s, axis=0, keepdims=True) + cr
                m_new = jnp.maximum(m_old, bm)
                alpha = jnp.exp2(m_old - m_new)
                ex_sc[c, :, sl] = jnp.maximum(ex_sc[c, :, sl], bm - m_old)
                l_sc[c, :, sl] = alpha * (l_sc[c, :, sl] + jnp.sum(p, axis=0, keepdims=True))
                acc_sc[c, :, sl] = alpha * (acc_sc[c, :, sl] + pv(j, p))
                m_sc[c, :, sl] = m_new

    def run(block_fn):
        def far(c_row):
            def body(j, carry):
                block_fn(j, None, c_row)
                return carry
            return body

        def near(j, carry):
            block_fn(j, bias_sc.at[j - (qi * r - 1)], None)
            return carry

        lax.fori_loop(0, lo, far(cfar_ref[0]), 0)
        lax.fori_loop(lo, hi, near, 0)
        lax.fori_loop(hi, nkv, far(cfar_ref[1]), 0)

    l_sc[...] = jnp.zeros(l_sc.shape, F32)
    acc_sc[...] = jnp.zeros(acc_sc.shape, F32)
    ex_sc[...] = jnp.zeros(ex_sc.shape, F32)

    @pl.when(qi == 0)
    def _():
        for c in range(2):
            for lc in range(chunks):
                m_sc[c, :, lc * w:(lc + 1) * w] = jnp.max(logits(0, bias_sc.at[1], c, lc), axis=0, keepdims=True)

    @pl.when(qi != 0)
    def _():
        for c in range(2):
            for lc in range(chunks):
                sl = slice(lc * w, (lc + 1) * w)
                m_sc[c, :, sl] = jnp.max(logits(0, None, c, lc), axis=0, keepdims=True) + cfar_ref[0][:, sl]

    run(fast_block)

    @pl.when(jnp.max(ex_sc[...]) > ATTN_EXP_GUARD)
    def _():
        m_sc[...] = jnp.full(m_sc.shape, NEG_BIG, F32)
        l_sc[...] = jnp.zeros(l_sc.shape, F32)
        acc_sc[...] = jnp.zeros(acc_sc.shape, F32)
        run(safe_block)

    lam =(jnp.exp(jnp.sum(lam_ref[0:1, :] * lam_ref[1:2, :], axis=-1, keepdims=True))
           - jnp.exp(jnp.sum(lam_ref[2:3, :] * lam_ref[3:4, :], axis=-1, keepdims=True))
           + LAM_INIT)
    o = acc_sc[0] / l_sc[0] - lam * (acc_sc[1] / l_sc[1])
    ms = jnp.mean(o * o, axis=0, keepdims=True)
    o = o * lax.rsqrt(ms + EPS) * g_ref[...]
    o = o * (1.0 - LAM_INIT)
    o_ref[...] = o.T.astype(BF16)


def _attention(qt, k, vt, wtab, cfar, lam_params, subln_g, b, s):
    tq, tk = ATTN_Q_TILE, ATTN_KV_TILE
    r = tq // tk
    nq = s // tq
    nkv = s // tk
    t = b * s
    return pl.pallas_call(
        functools.partial(_attn_kernel, nkv=nkv, tq=tq, tk=tk),
        grid=(b, N_HEADS, nq),
        in_specs=[
            pl.BlockSpec((4, HALF_DIM), lambda bi, h, qi: (0, 0)),
            pl.BlockSpec((tq // TOKEN_TILE, ATTN_HEAD, TOKEN_TILE), lambda bi, h, qi: (bi * nq + qi, h, 0)),
            pl.BlockSpec((s, ATTN_HEAD), lambda bi, h, qi: (bi, h)),
            pl.BlockSpec((s // TOKEN_TILE, ATTN_HEAD, TOKEN_TILE), lambda bi, h, qi: (bi, h, 0)),
            pl.BlockSpec((None, r + 2, 1, tq + tk), lambda bi, h, qi: (h, 0, 0, 0)),
            pl.BlockSpec((None, 2, 1, tq), lambda bi, h, qi: (h, 0, 0, 0)),
            pl.BlockSpec((ATTN_HEAD, 1), lambda bi, h, qi: (0, 0)),
        ],
        out_specs=pl.BlockSpec((tq, ATTN_HEAD), lambda bi, h, qi: (bi * nq + qi, h)),
        out_shape=jax.ShapeDtypeStruct((t, ATTN_WIDTH), BF16),
        scratch_shapes=[
            pltpu.VMEM((2, 1, tq), F32),
            pltpu.VMEM((2, 1, tq), F32),
            pltpu.VMEM((2, ATTN_HEAD, tq), F32),
            pltpu.VMEM((r + 2, tk, tq), F32),
            pltpu.VMEM((2, 1, tq), F32),
        ],
        compiler_params=_cparams(("arbitrary", "arbitrary", "arbitrary")),
        name="diff_attention",
    )(lam_params, qt, k, vt, wtab, cfar, subln_g.reshape(ATTN_HEAD, 1))


def _outproj_kernel(x_ref, yf_ref, ya_ref, w_ref, g_ref, x1_ref, h2_ref):
    y = (jnp.dot(yf_ref[...].astype(BF16), w_ref[:FOURIER_WIDTH, :], preferred_element_type=F32)
         + jnp.dot(ya_ref[...], w_ref[FOURIER_WIDTH:, :], preferred_element_type=F32))
    x1 = x_ref[...] + y
    x1_ref[...] = x1
    ms = jnp.mean(x1 * x1, axis=-1, keepdims=True)
    h2_ref[...] = (x1 * lax.rsqrt(ms + EPS) * g_ref[...]).astype(BF16)


def _outproj(x2d, yf, ya, w_out_bf, g2):
    t = x2d.shape[0]
    tm = TOKEN_TILE
    return pl.pallas_call(
        _outproj_kernel,
        grid=(t // tm,),
        in_specs=[
            pl.BlockSpec((tm, D_MODEL), lambda i: (i, 0)),
            pl.BlockSpec((tm, FOURIER_WIDTH), lambda i: (i, 0)),
            pl.BlockSpec((tm, ATTN_WIDTH), lambda i: (i, 0)),
            pl.BlockSpec((D_MODEL, D_MODEL), lambda i: (0, 0)),
            pl.BlockSpec((1, D_MODEL), lambda i: (0, 0)),
        ],
        out_specs=[
            pl.BlockSpec((tm, D_MODEL), lambda i: (i, 0)),
            pl.BlockSpec((tm, D_MODEL), lambda i: (i, 0)),
        ],
        out_shape=[
            jax.ShapeDtypeStruct((t, D_MODEL), F32),
            jax.ShapeDtypeStruct((t, D_MODEL), BF16),
        ],
        compiler_params=_cparams(("arbitrary",)),
        name="outproj",
    )(x2d, yf, ya, w_out_bf, g2.reshape(1, D_MODEL))


def _ffn_kernel(h_ref, x1_ref, wg_ref, wu_ref, wd_ref, g_ref, o_ref, acc_ref):
    f = pl.program_id(1)

    @pl.when(f == 0)
    def _():
        acc_ref[...] = jnp.zeros(acc_ref.shape, F32)

    h = h_ref[...]
    gate = jnp.dot(h, wg_ref[...], preferred_element_type=F32)
    up = jnp.dot(h, wu_ref[...], preferred_element_type=F32)
    a = (gate * (1.0 / (1.0 + jnp.exp(-gate))) * up).astype(BF16)
    acc_ref[...] += jnp.dot(a, wd_ref[...], preferred_element_type=F32)

    @pl.when(f == pl.num_programs(1) - 1)
    def _():
        x2 = x1_ref[...] + acc_ref[...]
        ms = jnp.mean(x2 * x2, axis=-1, keepdims=True)
        o_ref[...] = x2 * lax.rsqrt(ms + EPS) * g_ref[...]


def _ffn(h2, x1, wg_bf, wu_bf, wd_bf, gf):
    t = h2.shape[0]
    tm = TOKEN_TILE
    tf = FF_TILE
    return pl.pallas_call(
        _ffn_kernel,
        grid=(t // tm, D_FF // tf),
        in_specs=[
            pl.BlockSpec((tm, D_MODEL), lambda i, f: (i, 0)),
            pl.BlockSpec((tm, D_MODEL), lambda i, f: (i, 0)),
            pl.BlockSpec((D_MODEL, tf), lambda i, f: (0, f)),
            pl.BlockSpec((D_MODEL, tf), lambda i, f: (0, f)),
            pl.BlockSpec((tf, D_MODEL), lambda i, f: (f, 0)),
            pl.BlockSpec((1, D_MODEL), lambda i, f: (0, 0)),
        ],
        out_specs=pl.BlockSpec((tm, D_MODEL), lambda i, f: (i, 0)),
        out_shape=jax.ShapeDtypeStruct((t, D_MODEL), F32),
        scratch_shapes=[pltpu.VMEM((tm, D_MODEL), F32)],
        compiler_params=_cparams(("arbitrary", "arbitrary")),
        name="ffn",
    )(h2, x1, wg_bf, wu_bf, wd_bf, gf.reshape(1, D_MODEL))


def _encoder(x, params):
    b, s, _ = x.shape
    t = b * s
    n1 = s // SEQ_MINOR
    x2d = x.reshape(t, D_MODEL)

    pq, qt, k, vt = _inproj(x2d, params["norm1_g"], params["w_in"], params["ab"])
    y1 = _fourier1(pq.reshape(2, b, n1, SEQ_MINOR, FOURIER_WIDTH), n1)
    yf = _fourier2(y1, n1).reshape(t, FOURIER_WIDTH)
    ya = _attention(qt, k, vt, params["wtab"], params["cfar"], params["lam"], params["subln_g"], b, s)
    x1, h2 = _outproj(x2d, yf, ya, params["w_out"], params["norm2_g"])
    out = _ffn(h2, x1, params["w_gate"], params["w_up"], params["w_down"], params["final_g"])
    return out.reshape(b, s, D_MODEL)


def kernel(x_prompt, x_sample, norm1_g, w_in, w_fourier, lambda_q1, lambda_k1, lambda_q2, lambda_k2,
           subln_g, w_out, norm2_g, w_gate, w_up, w_down, rel_bias, final_g):
    wtab, cfar = _bias_tables(rel_bias)
    params = {
        "norm1_g": norm1_g[0].astype(F32),
        "w_in": w_in[0].astype(BF16),
        "ab": _fourier_weights(w_fourier[0].astype(F32)),
        "lam": jnp.stack([lambda_q1[0], lambda_k1[0], lambda_q2[0], lambda_k2[0]]).astype(F32),
        "subln_g": subln_g[0].astype(F32),
        "w_out": w_out[0].astype(BF16),
        "norm2_g": norm2_g[0].astype(F32),
        "w_gate": w_gate[0].astype(BF16),
        "w_up": w_up[0].astype(BF16),
        "w_down": w_down[0].astype(BF16),
        "wtab": wtab,
        "cfar": cfar,
        "final_g": final_g.astype(F32),
    }
    return (_encoder(x_prompt, params), _encoder(x_sample, params))
```

```python
import functools
import math

import numpy as np
import jax
import jax.numpy as jnp
from jax import lax
from jax.experimental import pallas as pl
from jax.experimental.pallas import tpu as pltpu

F32 = jnp.float32
BF16 = jnp.bfloat16

D_MODEL = 2048
FOURIER_WIDTH = 1024
FOURIER_GROUP = 128
N_FGROUPS = 8
ATTN_WIDTH = 1024
ATTN_HEAD = 128
N_HEADS = 8
HALF_DIM = 64
D_FF = 5632
NUM_BUCKETS = 32
MAX_DISTANCE = 128
EPS = 1e-6
LAM_INIT = 0.8 - 0.6 * math.exp(-0.3 * 0)

SEQ_MINOR = 128
TOKEN_TILE = 512
FF_TILE = 512
ATTN_Q_TILE = 1024
ATTN_KV_TILE = 1024
ATTN_EXP_GUARD = 64.0
ATTN_LANE_CHUNK = 256
LOG2E = math.log2(math.e)
NEG_BIG = -1e30
VMEM_LIMIT = 56 * 1024 * 1024


def _cparams(sem):
    return pltpu.CompilerParams(dimension_semantics=sem, vmem_limit_bytes=VMEM_LIMIT)


def _dft128_tables():
    n = np.arange(FOURIER_GROUP)
    ang = 2.0 * np.pi * ((n[:, None] * n[None, :]) % FOURIER_GROUP) / FOURIER_GROUP
    return np.stack([np.cos(ang), np.sin(ang)]).astype(np.float32)


def _stage1_matrix(n1):
    n = np.arange(n1)
    ang = 2.0 * np.pi * ((n[:, None] * n[None, :]) % n1) / n1
    c, s = np.cos(ang), np.sin(ang)
    return np.block([[c, -s], [-s, -c]]).astype(np.float32)


def _stage2_matrices(n1):
    s_len = n1 * SEQ_MINOR
    s1p = np.arange(n1)[:, None, None]
    s2p = np.arange(SEQ_MINOR)[None, :, None]
    s2 = np.arange(SEQ_MINOR)[None, None, :]
    ang = 2.0 * np.pi * ((s2 * (s1p + n1 * s2p)) % s_len) / s_len
    scale = 1.0 / math.sqrt(s_len * FOURIER_GROUP)
    return (np.concatenate([np.cos(ang), np.sin(ang)], axis=-1) * scale).astype(np.float32)


def _fourier_weight_kernel(dft_ref, w_ref, ab_ref):
    for g in range(N_FGROUPS):
        w = w_ref[g]
        a = jnp.dot(dft_ref[0], w, preferred_element_type=F32, precision=lax.Precision.HIGHEST)
        b = jnp.dot(dft_ref[1], w, preferred_element_type=F32, precision=lax.Precision.HIGHEST)
        ab_ref[g, :, :FOURIER_GROUP] = a.astype(BF16)
        ab_ref[g, :, FOURIER_GROUP:] = b.astype(BF16)


def _fourier_weights(w_f):
    return pl.pallas_call(
        _fourier_weight_kernel,
        out_shape=jax.ShapeDtypeStruct((N_FGROUPS, FOURIER_GROUP, 2 * FOURIER_GROUP), BF16),
        name="fourier_weights",
    )(jnp.asarray(_dft128_tables()), w_f)


def _inproj_kernel(x_ref, g_ref, w_ref, ab_ref, pq_ref, qt_ref, k_ref, vt_ref):
    x = x_ref[...]
    ms = jnp.mean(x * x, axis=-1, keepdims=True)
    h = (x * lax.rsqrt(ms + EPS) * g_ref[...]).astype(BF16)

    def proj(col):
        return jnp.dot(h, w_ref[:, col:col + ATTN_WIDTH], preferred_element_type=F32)

    ub = proj(0).astype(BF16)
    for g in range(N_FGROUPS):
        lo, hi = g * FOURIER_GROUP, (g + 1) * FOURIER_GROUP
        pq = jnp.dot(ub[:, lo:hi], ab_ref[g], preferred_element_type=F32)
        pq_ref[0, :, lo:hi] = pq[:, :FOURIER_GROUP].astype(BF16)
        pq_ref[1, :, lo:hi] = pq[:, FOURIER_GROUP:].astype(BF16)
    qt_ref[0] = (proj(FOURIER_WIDTH) * (HALF_DIM ** -0.5 * LOG2E)).T.astype(BF16)
    k_ref[...] = proj(FOURIER_WIDTH + ATTN_WIDTH).astype(BF16)
    vt_ref[0] = proj(FOURIER_WIDTH + 2 * ATTN_WIDTH).T.astype(BF16)


def _inproj(x2d, g, w_in_bf, ab):
    t = x2d.shape[0]
    tm = TOKEN_TILE
    nt = t // tm
    return pl.pallas_call(
        _inproj_kernel,
        grid=(nt,),
        in_specs=[
            pl.BlockSpec((tm, D_MODEL), lambda i: (i, 0)),
            pl.BlockSpec((1, D_MODEL), lambda i: (0, 0)),
            pl.BlockSpec((D_MODEL, FOURIER_WIDTH + 3 * ATTN_WIDTH), lambda i: (0, 0), pipeline_mode=pl.Buffered(1)),
            pl.BlockSpec((N_FGROUPS, FOURIER_GROUP, 2 * FOURIER_GROUP), lambda i: (0, 0, 0)),
        ],
        out_specs=[
            pl.BlockSpec((2, tm, FOURIER_WIDTH), lambda i: (0, i, 0)),
            pl.BlockSpec((1, ATTN_WIDTH, tm), lambda i: (i, 0, 0)),
            pl.BlockSpec((tm, ATTN_WIDTH), lambda i: (i, 0)),
            pl.BlockSpec((1, ATTN_WIDTH, tm), lambda i: (i, 0, 0)),
        ],
        out_shape=[
            jax.ShapeDtypeStruct((2, t, FOURIER_WIDTH), BF16),
            jax.ShapeDtypeStruct((nt, ATTN_WIDTH, tm), BF16),
            jax.ShapeDtypeStruct((t, ATTN_WIDTH), BF16),
            jax.ShapeDtypeStruct((nt, ATTN_WIDTH, tm), BF16),
        ],
        compiler_params=_cparams(("arbitrary",)),
        name="inproj",
    )(x2d, g.reshape(1, D_MODEL), w_in_bf, ab)


def _fourier1_kernel(x_ref, d_ref, o_ref, *, n1, rows):
    x = pltpu.einshape("abc->bac", x_ref[...].reshape(2 * n1, rows, FOURIER_WIDTH))
    d = d_ref[...]
    for r in range(rows):
        y = jnp.dot(d, x[r], preferred_element_type=F32)
        o_ref[0, r] = y[:n1].astype(BF16)
        o_ref[1, r] = y[n1:].astype(BF16)


def _fourier1(pq5, n1):
    b = pq5.shape[1]
    rows = 16 if n1 >= 64 else 64
    d1 = jnp.asarray(_stage1_matrix(n1)).astype(BF16)
    return pl.pallas_call(
        functools.partial(_fourier1_kernel, n1=n1, rows=rows),
        grid=(b, SEQ_MINOR // rows),
        in_specs=[
            pl.BlockSpec((2, None, n1, rows, FOURIER_WIDTH), lambda i, j: (0, i, 0, j, 0)),
            pl.BlockSpec((2 * n1, 2 * n1), lambda i, j: (0, 0)),
        ],
        out_specs=pl.BlockSpec((None, 2, rows, n1, FOURIER_WIDTH), lambda i, j: (i, 0, j, 0, 0)),
        out_shape=jax.ShapeDtypeStruct((b, 2, SEQ_MINOR, n1, FOURIER_WIDTH), BF16),
        compiler_params=_cparams(("arbitrary", "arbitrary")),
        name="fourier_stage1",
    )(pq5, d1)


def _fourier2_kernel(x_ref, m_ref, o_ref, *, rows):
    x = pltpu.einshape("abc->bac", x_ref[...].reshape(2 * SEQ_MINOR, rows, FOURIER_WIDTH))
    ys = [jnp.dot(m_ref[i], x[i], preferred_element_type=F32).astype(BF16) for i in range(rows)]
    o_ref[...] = pltpu.einshape("abc->bac", jnp.stack(ys))


def _fourier2(y1, n1):
    b = y1.shape[0]
    rows = 16
    m2 = jnp.asarray(_stage2_matrices(n1)).astype(BF16)
    return pl.pallas_call(
        functools.partial(_fourier2_kernel, rows=rows),
        grid=(b, n1 // rows),
        in_specs=[
            pl.BlockSpec((None, 2, SEQ_MINOR, rows, FOURIER_WIDTH), lambda i, j: (i, 0, 0, j, 0)),
            pl.BlockSpec((rows, SEQ_MINOR, 2 * SEQ_MINOR), lambda i, j: (j, 0, 0)),
        ],
        out_specs=pl.BlockSpec((None, SEQ_MINOR, rows, FOURIER_WIDTH), lambda i, j: (i, 0, j, 0)),
        out_shape=jax.ShapeDtypeStruct((b, SEQ_MINOR, n1, FOURIER_WIDTH), BF16),
        compiler_params=_cparams(("arbitrary", "arbitrary")),
        name="fourier_stage2",
    )(y1, m2)


def _t5_bucket(rel):
    nb = NUM_BUCKETS // 2
    max_exact = nb // 2
    ret = (rel > 0).astype(jnp.int32) * nb
    n = jnp.abs(rel)
    nf = jnp.maximum(n, 1).astype(jnp.float32)
    large = max_exact + (jnp.log(nf / max_exact) / math.log(MAX_DISTANCE / max_exact)
                         * (nb - max_exact)).astype(jnp.int32)
    large = jnp.minimum(large, nb - 1)
    return ret + jnp.where(n < max_exact, n, large)


def _bias_tables(rel_bias):
    tq, tk = ATTN_Q_TILE, ATTN_KV_TILE
    r = tq // tk
    wd = tq + tk
    mp = np.arange(wd)
    m = np.where(mp < tq, mp, mp - wd)
    d = (np.arange(r + 2)[:, None] - 1) * tk - m[None, :]
    tbl = rel_bias.astype(F32) * LOG2E
    w = tbl[_t5_bucket(jnp.asarray(d, jnp.int32))]
    wtab = jnp.transpose(w, (2, 0, 1))[:, :, None, :]
    far = tbl[_t5_bucket(jnp.asarray([-2 * tk, 2 * tk], jnp.int32))]
    cfar = jnp.broadcast_to(far.T[:, :, None, None], (N_HEADS, 2, 1, tq))
    return wtab, cfar


def _attn_kernel(lam_ref, qt_ref, k_ref, vt_ref, w_ref, cfar_ref, g_ref, o_ref, m_sc, l_sc, acc_sc, bias_sc, ex_sc,
                 *, nkv, tq, tk):
    qi = pl.program_id(2)
    r = tq // tk
    w = ATTN_LANE_CHUNK
    chunks = tq // w
    pieces = tk // TOKEN_TILE

    @pl.when((pl.program_id(1) == 0) & (qi == 0))
    def _():
        for t in range(r + 2):
            x = jnp.broadcast_to(w_ref[t], (tk, tq + tk))
            bias_sc[t] = pltpu.roll(x, 0, 1, stride=1, stride_axis=0)[:, :tq]

    row = lax.broadcasted_iota(jnp.int32, (ATTN_HEAD, w), 0)
    qts = []
    for lc in range(chunks):
        q = qt_ref[(lc * w) // TOKEN_TILE][:, (lc * w) % TOKEN_TILE:(lc * w) % TOKEN_TILE + w]
        zero = jnp.zeros_like(q)
        qts.append((jnp.where(row < HALF_DIM, q, zero), jnp.where(row >= HALF_DIM, q, zero)))

    lo = jnp.maximum(qi * r - 1, 0)
    hi = jnp.minimum((qi + 1) * r + 1, nkv)

    def logits(j, bias_view, c, lc):
        kj = k_ref[pl.ds(pl.multiple_of(j * tk, tk), tk), :]
        s = jnp.dot(kj, qts[lc][c], preferred_element_type=F32)
        if bias_view is not None:
            s = jnp.maximum(s, NEG_BIG) + bias_view[:, lc * w:(lc + 1) * w]
        return s

    def pv(j, p):
        pb = p.astype(BF16)
        out = jnp.dot(vt_ref[j * pieces], pb[:TOKEN_TILE], preferred_element_type=F32)
        for i in range(1, pieces):
            out += jnp.dot(vt_ref[j * pieces + i], pb[i * TOKEN_TILE:(i + 1) * TOKEN_TILE],
                           preferred_element_type=F32)
        return out

    def safe_block(j, bias_view, c_row):
        ss = [[logits(j, bias_view, c, lc) for lc in range(chunks)] for c in range(2)]
        for c in range(2):
            for lc in range(chunks):
                sl = slice(lc * w, (lc + 1) * w)
                s = ss[c][lc]
                m_old = m_sc[c, :, sl]
                if bias_view is not None:
                    m_new = jnp.maximum(m_old, jnp.max(s, axis=0, keepdims=True))
                    p = jnp.exp2(s - m_new)
                else:
                    cr = c_row[:, sl]
                    m_new = jnp.maximum(m_old, jnp.max(s, axis=0, keepdims=True) + cr)
                    p = jnp.exp2(s - (m_new - cr))
                alpha = jnp.exp2(m_old - m_new)
                l_sc[c, :, sl] = alpha * l_sc[c, :, sl] + jnp.sum(p, axis=0, keepdims=True)
                acc_sc[c, :, sl] = alpha * acc_sc[c, :, sl] + pv(j, p)
                m_sc[c, :, sl] = m_new

    def fast_block(j, bias_view, c_row):
        ss = [[logits(j, bias_view, c, lc) for lc in range(chunks)] for c in range(2)]
        for c in range(2):
            for lc in range(chunks):
                sl = slice(lc * w, (lc + 1) * w)
                s = ss[c][lc]
                m_old = m_sc[c, :, sl]
                if bias_view is not None:
                    p = jnp.exp2(s - m_old)
                    bm = jnp.max(s, axis=0, keepdims=True)
                else:
                    cr = c_row[:, sl]
                    p = jnp.exp2(s - (m_old - cr))
                    bm = jnp.max(s, axis=0, keepdims=True) + cr
                m_new = jnp.maximum(m_old, bm)
                alpha = jnp.exp2(m_old - m_new)
                ex_sc[c, :, sl] = jnp.maximum(ex_sc[c, :, sl], bm - m_old)
                l_sc[c, :, sl] = alpha * (l_sc[c, :, sl] + jnp.sum(p, axis=0, keepdims=True))
                acc_sc[c, :, sl] = alpha * (acc_sc[c, :, sl] + pv(j, p))
                m_sc[c, :, sl] = m_new

    def run(block_fn):
        def far(c_row):
            def body(j, carry):
                block_fn(j, None, c_row)
                return carry
            return body

        def near(j, carry):
            block_fn(j, bias_sc.at[j - (qi * r - 1)], None)
            return carry

        lax.fori_loop(0, lo, far(cfar_ref[0]), 0)
        lax.fori_loop(lo, hi, near, 0)
        lax.fori_loop(hi, nkv, far(cfar_ref[1]), 0)

    l_sc[...] = jnp.zeros(l_sc.shape, F32)
    acc_sc[...] = jnp.zeros(acc_sc.shape, F32)
    ex_sc[...] = jnp.zeros(ex_sc.shape, F32)

    @pl.when(qi == 0)
    def _():
        for c in range(2):
            for lc in range(chunks):
                m_sc[c, :, lc * w:(lc + 1) * w] = jnp.max(logits(0, bias_sc.at[1], c, lc), axis=0, keepdims=True)

    @pl.when(qi != 0)
    def _():
        for c in range(2):
            for lc in range(chunks):
                sl = slice(lc * w, (lc + 1) * w)
                m_sc[c, :, sl] = jnp.max(logits(0, None, c, lc), axis=0, keepdims=True) + cfar_ref[0][:, sl]

    run(fast_block)

    @pl.when(jnp.max(ex_sc[...]) > ATTN_EXP_GUARD)
    def _():
        m_sc[...] = jnp.full(m_sc.shape, NEG_BIG, F32)
        l_sc[...] = jnp.zeros(l_sc.shape, F32)
        acc_sc[...] = jnp.zeros(acc_sc.shape, F32)
        run(safe_block)

    lam =(jnp.exp(jnp.sum(lam_ref[0:1, :] * lam_ref[1:2, :], axis=-1, keepdims=True))
           - jnp.exp(jnp.sum(lam_ref[2:3, :] * lam_ref[3:4, :], axis=-1, keepdims=True))
           + LAM_INIT)
    o = acc_sc[0] / l_sc[0] - lam * (acc_sc[1] / l_sc[1])
    ms = jnp.mean(o * o, axis=0, keepdims=True)
    o = o * lax.rsqrt(ms + EPS) * g_ref[...]
    o = o * (1.0 - LAM_INIT)
    o_ref[...] = o.T.astype(BF16)


def _attention(qt, k, vt, wtab, cfar, lam_params, subln_g, b, s):
    tq, tk = ATTN_Q_TILE, ATTN_KV_TILE
    r = tq // tk
    nq = s // tq
    nkv = s // tk
    t = b * s
    return pl.pallas_call(
        functools.partial(_attn_kernel, nkv=nkv, tq=tq, tk=tk),
        grid=(N_HEADS, b, nq),
        in_specs=[
            pl.BlockSpec((4, HALF_DIM), lambda h, bi, qi: (0, 0)),
            pl.BlockSpec((tq // TOKEN_TILE, ATTN_HEAD, TOKEN_TILE), lambda h, bi, qi: (bi * nq + qi, h, 0)),
            pl.BlockSpec((s, ATTN_HEAD), lambda h, bi, qi: (bi, h)),
            pl.BlockSpec((s // TOKEN_TILE, ATTN_HEAD, TOKEN_TILE), lambda h, bi, qi: (bi, h, 0)),
            pl.BlockSpec((None, r + 2, 1, tq + tk), lambda h, bi, qi: (h, 0, 0, 0)),
            pl.BlockSpec((None, 2, 1, tq), lambda h, bi, qi: (h, 0, 0, 0)),
            pl.BlockSpec((ATTN_HEAD, 1), lambda h, bi, qi: (0, 0)),
        ],
        out_specs=pl.BlockSpec((tq, ATTN_HEAD), lambda h, bi, qi: (bi * nq + qi, h)),
        out_shape=jax.ShapeDtypeStruct((t, ATTN_WIDTH), BF16),
        scratch_shapes=[
            pltpu.VMEM((2, 1, tq), F32),
            pltpu.VMEM((2, 1, tq), F32),
            pltpu.VMEM((2, ATTN_HEAD, tq), F32),
            pltpu.VMEM((r + 2, tk, tq), F32),
            pltpu.VMEM((2, 1, tq), F32),
        ],
        compiler_params=_cparams(("arbitrary", "arbitrary", "arbitrary")),
        name="diff_attention",
    )(lam_params, qt, k, vt, wtab, cfar, subln_g.reshape(ATTN_HEAD, 1))


def _outproj_kernel(x_ref, yf_ref, ya_ref, w_ref, g_ref, x1_ref, h2_ref):
    y = (jnp.dot(yf_ref[...], w_ref[:FOURIER_WIDTH, :], preferred_element_type=F32)
         + jnp.dot(ya_ref[...], w_ref[FOURIER_WIDTH:, :], preferred_element_type=F32))
    x1 = x_ref[...] + y
    x1_ref[...] = x1
    ms = jnp.mean(x1 * x1, axis=-1, keepdims=True)
    h2_ref[...] = (x1 * lax.rsqrt(ms + EPS) * g_ref[...]).astype(BF16)


def _outproj(x2d, yf, ya, w_out_bf, g2):
    t = x2d.shape[0]
    tm = TOKEN_TILE
    return pl.pallas_call(
        _outproj_kernel,
        grid=(t // tm,),
        in_specs=[
            pl.BlockSpec((tm, D_MODEL), lambda i: (i, 0)),
            pl.BlockSpec((tm, FOURIER_WIDTH), lambda i: (i, 0)),
            pl.BlockSpec((tm, ATTN_WIDTH), lambda i: (i, 0)),
            pl.BlockSpec((D_MODEL, D_MODEL), lambda i: (0, 0)),
            pl.BlockSpec((1, D_MODEL), lambda i: (0, 0)),
        ],
        out_specs=[
            pl.BlockSpec((tm, D_MODEL), lambda i: (i, 0)),
            pl.BlockSpec((tm, D_MODEL), lambda i: (i, 0)),
        ],
        out_shape=[
            jax.ShapeDtypeStruct((t, D_MODEL), F32),
            jax.ShapeDtypeStruct((t, D_MODEL), BF16),
        ],
        compiler_params=_cparams(("arbitrary",)),
        name="outproj",
    )(x2d, yf, ya, w_out_bf, g2.reshape(1, D_MODEL))


def _ffn_kernel(h_ref, x1_ref, wg_ref, wu_ref, wd_ref, g_ref, o_ref, acc_ref):
    f = pl.program_id(1)

    @pl.when(f == 0)
    def _():
        acc_ref[...] = jnp.zeros(acc_ref.shape, F32)

    h = h_ref[...]
    gate = jnp.dot(h, wg_ref[...], preferred_element_type=F32)
    up = jnp.dot(h, wu_ref[...], preferred_element_type=F32)
    a = (gate * (1.0 / (1.0 + jnp.exp(-gate))) * up).astype(BF16)
    acc_ref[...] += jnp.dot(a, wd_ref[...], preferred_element_type=F32)

    @pl.when(f == pl.num_programs(1) - 1)
    def _():
        x2 = x1_ref[...] + acc_ref[...]
        ms = jnp.mean(x2 * x2, axis=-1, keepdims=True)
        o_ref[...] = x2 * lax.rsqrt(ms + EPS) * g_ref[...]


def _ffn(h2, x1, wg_bf, wu_bf, wd_bf, gf):
    t = h2.shape[0]
    tm = TOKEN_TILE
    tf = FF_TILE
    return pl.pallas_call(
        _ffn_kernel,
        grid=(t // tm, D_FF // tf),
        in_specs=[
            pl.BlockSpec((tm, D_MODEL), lambda i, f: (i, 0)),
            pl.BlockSpec((tm, D_MODEL), lambda i, f: (i, 0)),
            pl.BlockSpec((D_MODEL, tf), lambda i, f: (0, f)),
            pl.BlockSpec((D_MODEL, tf), lambda i, f: (0, f)),
            pl.BlockSpec((tf, D_MODEL), lambda i, f: (f, 0)),
            pl.BlockSpec((1, D_MODEL), lambda i, f: (0, 0)),
        ],
        out_specs=pl.BlockSpec((tm, D_MODEL), lambda i, f: (i, 0)),
        out_shape=jax.ShapeDtypeStruct((t, D_MODEL), F32),
        scratch_shapes=[pltpu.VMEM((tm, D_MODEL), F32)],
        compiler_params=_cparams(("arbitrary", "arbitrary")),
        name="ffn",
    )(h2, x1, wg_bf, wu_bf, wd_bf, gf.reshape(1, D_MODEL))


def _encoder(x, params):
    b, s, _ = x.shape
    t = b * s
    n1 = s // SEQ_MINOR
    x2d = x.reshape(t, D_MODEL)

    pq, qt, k, vt = _inproj(x2d, params["norm1_g"], params["w_in"], params["ab"])
    y1 = _fourier1(pq.reshape(2, b, n1, SEQ_MINOR, FOURIER_WIDTH), n1)
    yf = _fourier2(y1, n1).reshape(t, FOURIER_WIDTH)
    ya = _attention(qt, k, vt, params["wtab"], params["cfar"], params["lam"], params["subln_g"], b, s)
    x1, h2 = _outproj(x2d, yf, ya, params["w_out"], params["norm2_g"])
    out = _ffn(h2, x1, params["w_gate"], params["w_up"], params["w_down"], params["final_g"])
    return out.reshape(b, s, D_MODEL)


def kernel(x_prompt, x_sample, norm1_g, w_in, w_fourier, lambda_q1, lambda_k1, lambda_q2, lambda_k2,
           subln_g, w_out, norm2_g, w_gate, w_up, w_down, rel_bias, final_g):
    wtab, cfar = _bias_tables(rel_bias)
    params = {
        "norm1_g": norm1_g[0].astype(F32),
        "w_in": w_in[0].astype(BF16),
        "ab": _fourier_weights(w_fourier[0].astype(F32)),
        "lam": jnp.stack([lambda_q1[0], lambda_k1[0], lambda_q2[0], lambda_k2[0]]).astype(F32),
        "subln_g": subln_g[0].astype(F32),
        "w_out": w_out[0].astype(BF16),
        "norm2_g": norm2_g[0].astype(F32),
        "w_gate": w_gate[0].astype(BF16),
        "w_up": w_up[0].astype(BF16),
        "w_down": w_down[0].astype(BF16),
        "wtab": wtab,
        "cfar": cfar,
        "final_g": final_g.astype(F32),
    }
    return (_encoder(x_prompt, params), _encoder(x_sample, params))
```

```python
import functools
import math

import numpy as np
import jax
import jax.numpy as jnp
from jax import lax
from jax.experimental import pallas as pl
from jax.experimental.pallas import tpu as pltpu

F32 = jnp.float32
BF16 = jnp.bfloat16

D_MODEL = 2048
FOURIER_WIDTH = 1024
FOURIER_GROUP = 128
N_FGROUPS = 8
ATTN_WIDTH = 1024
ATTN_HEAD = 128
N_HEADS = 8
HALF_DIM = 64
D_FF = 5632
NUM_BUCKETS = 32
MAX_DISTANCE = 128
EPS = 1e-6
LAM_INIT = 0.8 - 0.6 * math.exp(-0.3 * 0)

SEQ_MINOR = 128
TOKEN_TILE = 512
FF_TILE = 512
ATTN_Q_TILE = 1024
ATTN_KV_TILE = 1024
ATTN_EXP_GUARD = 64.0
ATTN_LEVEL_ROWS = 256
ATTN_LANE_CHUNK = 256
LOG2E = math.log2(math.e)
NEG_BIG = -1e30
VMEM_LIMIT = 56 * 1024 * 1024


def _cparams(sem):
    return pltpu.CompilerParams(dimension_semantics=sem, vmem_limit_bytes=VMEM_LIMIT)


def _dft128_tables():
    n = np.arange(FOURIER_GROUP)
    ang = 2.0 * np.pi * ((n[:, None] * n[None, :]) % FOURIER_GROUP) / FOURIER_GROUP
    return np.stack([np.cos(ang), np.sin(ang)]).astype(np.float32)


def _stage1_matrix(n1):
    n = np.arange(n1)
    ang = 2.0 * np.pi * ((n[:, None] * n[None, :]) % n1) / n1
    c, s = np.cos(ang), np.sin(ang)
    return np.block([[c, -s], [-s, -c]]).astype(np.float32)


def _stage2_matrices(n1):
    s_len = n1 * SEQ_MINOR
    s1p = np.arange(n1)[:, None, None]
    s2p = np.arange(SEQ_MINOR)[None, :, None]
    s2 = np.arange(SEQ_MINOR)[None, None, :]
    ang = 2.0 * np.pi * ((s2 * (s1p + n1 * s2p)) % s_len) / s_len
    scale = 1.0 / math.sqrt(s_len * FOURIER_GROUP)
    return (np.concatenate([np.cos(ang), np.sin(ang)], axis=-1) * scale).astype(np.float32)


def _fourier_weight_kernel(dft_ref, w_ref, ab_ref):
    for g in range(N_FGROUPS):
        w = w_ref[g]
        a = jnp.dot(dft_ref[0], w, preferred_element_type=F32, precision=lax.Precision.HIGHEST)
        b = jnp.dot(dft_ref[1], w, preferred_element_type=F32, precision=lax.Precision.HIGHEST)
        ab_ref[g, :, :FOURIER_GROUP] = a.astype(BF16)
        ab_ref[g, :, FOURIER_GROUP:] = b.astype(BF16)


def _fourier_weights(w_f):
    return pl.pallas_call(
        _fourier_weight_kernel,
        out_shape=jax.ShapeDtypeStruct((N_FGROUPS, FOURIER_GROUP, 2 * FOURIER_GROUP), BF16),
        name="fourier_weights",
    )(jnp.asarray(_dft128_tables()), w_f)


def _inproj_kernel(x_ref, g_ref, w_ref, ab_ref, pq_ref, qt_ref, k_ref, vt_ref):
    x = x_ref[...]
    ms = jnp.mean(x * x, axis=-1, keepdims=True)
    h = (x * lax.rsqrt(ms + EPS) * g_ref[...]).astype(BF16)

    def proj(col):
        return jnp.dot(h, w_ref[:, col:col + ATTN_WIDTH], preferred_element_type=F32)

    ub = proj(0).astype(BF16)
    for g in range(N_FGROUPS):
        lo, hi = g * FOURIER_GROUP, (g + 1) * FOURIER_GROUP
        pq = jnp.dot(ub[:, lo:hi], ab_ref[g], preferred_element_type=F32)
        pq_ref[0, :, lo:hi] = pq[:, :FOURIER_GROUP].astype(BF16)
        pq_ref[1, :, lo:hi] = pq[:, FOURIER_GROUP:].astype(BF16)
    qt_ref[0] = (proj(FOURIER_WIDTH) * (HALF_DIM ** -0.5 * LOG2E)).T.astype(BF16)
    k_ref[...] = proj(FOURIER_WIDTH + ATTN_WIDTH).astype(BF16)
    vt_ref[0] = proj(FOURIER_WIDTH + 2 * ATTN_WIDTH).T.astype(BF16)


def _inproj(x2d, g, w_in_bf, ab):
    t = x2d.shape[0]
    tm = TOKEN_TILE
    nt = t // tm
    return pl.pallas_call(
        _inproj_kernel,
        grid=(nt,),
        in_specs=[
            pl.BlockSpec((tm, D_MODEL), lambda i: (i, 0)),
            pl.BlockSpec((1, D_MODEL), lambda i: (0, 0)),
            pl.BlockSpec((D_MODEL, FOURIER_WIDTH + 3 * ATTN_WIDTH), lambda i: (0, 0), pipeline_mode=pl.Buffered(1)),
            pl.BlockSpec((N_FGROUPS, FOURIER_GROUP, 2 * FOURIER_GROUP), lambda i: (0, 0, 0)),
        ],
        out_specs=[
            pl.BlockSpec((2, tm, FOURIER_WIDTH), lambda i: (0, i, 0)),
            pl.BlockSpec((1, ATTN_WIDTH, tm), lambda i: (i, 0, 0)),
            pl.BlockSpec((tm, ATTN_WIDTH), lambda i: (i, 0)),
            pl.BlockSpec((1, ATTN_WIDTH, tm), lambda i: (i, 0, 0)),
        ],
        out_shape=[
            jax.ShapeDtypeStruct((2, t, FOURIER_WIDTH), BF16),
            jax.ShapeDtypeStruct((nt, ATTN_WIDTH, tm), BF16),
            jax.ShapeDtypeStruct((t, ATTN_WIDTH), BF16),
            jax.ShapeDtypeStruct((nt, ATTN_WIDTH, tm), BF16),
        ],
        compiler_params=_cparams(("arbitrary",)),
        name="inproj",
    )(x2d, g.reshape(1, D_MODEL), w_in_bf, ab)


def _fourier1_kernel(x_ref, d_ref, o_ref, *, n1, rows):
    x = pltpu.einshape("abc->bac", x_ref[...].reshape(2 * n1, rows, FOURIER_WIDTH))
    d = d_ref[...]
    for r in range(rows):
        y = jnp.dot(d, x[r], preferred_element_type=F32)
        o_ref[0, r] = y[:n1].astype(BF16)
        o_ref[1, r] = y[n1:].astype(BF16)


def _fourier1(pq5, n1):
    b = pq5.shape[1]
    rows = 16 if n1 >= 64 else 64
    d1 = jnp.asarray(_stage1_matrix(n1)).astype(BF16)
    return pl.pallas_call(
        functools.partial(_fourier1_kernel, n1=n1, rows=rows),
        grid=(b, SEQ_MINOR // rows),
        in_specs=[
            pl.BlockSpec((2, None, n1, rows, FOURIER_WIDTH), lambda i, j: (0, i, 0, j, 0)),
            pl.BlockSpec((2 * n1, 2 * n1), lambda i, j: (0, 0)),
        ],
        out_specs=pl.BlockSpec((None, 2, rows, n1, FOURIER_WIDTH), lambda i, j: (i, 0, j, 0, 0)),
        out_shape=jax.ShapeDtypeStruct((b, 2, SEQ_MINOR, n1, FOURIER_WIDTH), BF16),
        compiler_params=_cparams(("arbitrary", "arbitrary")),
        name="fourier_stage1",
    )(pq5, d1)


def _fourier2_kernel(x_ref, m_ref, o_ref, *, rows):
    x = pltpu.einshape("abc->bac", x_ref[...].reshape(2 * SEQ_MINOR, rows, FOURIER_WIDTH))
    ys = [jnp.dot(m_ref[i], x[i], preferred_element_type=F32).astype(BF16) for i in range(rows)]
    o_ref[...] = pltpu.einshape("abc->bac", jnp.stack(ys))


def _fourier2(y1, n1):
    b = y1.shape[0]
    rows = 16
    m2 = jnp.asarray(_stage2_matrices(n1)).astype(BF16)
    return pl.pallas_call(
        functools.partial(_fourier2_kernel, rows=rows),
        grid=(b, n1 // rows),
        in_specs=[
            pl.BlockSpec((None, 2, SEQ_MINOR, rows, FOURIER_WIDTH), lambda i, j: (i, 0, 0, j, 0)),
            pl.BlockSpec((rows, SEQ_MINOR, 2 * SEQ_MINOR), lambda i, j: (j, 0, 0)),
        ],
        out_specs=pl.BlockSpec((None, SEQ_MINOR, rows, FOURIER_WIDTH), lambda i, j: (i, 0, j, 0)),
        out_shape=jax.ShapeDtypeStruct((b, SEQ_MINOR, n1, FOURIER_WIDTH), BF16),
        compiler_params=_cparams(("arbitrary", "arbitrary")),
        name="fourier_stage2",
    )(y1, m2)


def _t5_bucket(rel):
    nb = NUM_BUCKETS // 2
    max_exact = nb // 2
    ret = (rel > 0).astype(jnp.int32) * nb
    n = jnp.abs(rel)
    nf = jnp.maximum(n, 1).astype(jnp.float32)
    large = max_exact + (jnp.log(nf / max_exact) / math.log(MAX_DISTANCE / max_exact)
                         * (nb - max_exact)).astype(jnp.int32)
    large = jnp.minimum(large, nb - 1)
    return ret + jnp.where(n < max_exact, n, large)


def _bias_tables(rel_bias):
    tq, tk = ATTN_Q_TILE, ATTN_KV_TILE
    r = tq // tk
    wd = tq + tk
    mp = np.arange(wd)
    m = np.where(mp < tq, mp, mp - wd)
    d = (np.arange(r + 2)[:, None] - 1) * tk - m[None, :]
    tbl = rel_bias.astype(F32) * LOG2E
    w = tbl[_t5_bucket(jnp.asarray(d, jnp.int32))]
    wtab = jnp.transpose(w, (2, 0, 1))[:, :, None, :]
    far = tbl[_t5_bucket(jnp.asarray([-2 * tk, 2 * tk], jnp.int32))]
    cfar = jnp.broadcast_to(far.T[:, :, None, None], (N_HEADS, 2, 1, tq))
    return wtab, cfar


def _attn_kernel(lam_ref, qt_ref, k_ref, vt_ref, w_ref, cfar_ref, g_ref, o_ref, m_sc, l_sc, acc_sc, bias_sc, ex_sc,
                 *, nkv, tq, tk):
    qi = pl.program_id(2)
    r = tq // tk
    w = ATTN_LANE_CHUNK
    chunks = tq // w
    pieces = tk // TOKEN_TILE

    @pl.when((pl.program_id(1) == 0) & (qi == 0))
    def _():
        for t in range(r + 2):
            x = jnp.broadcast_to(w_ref[t], (tk, tq + tk))
            bias_sc[t] = pltpu.roll(x, 0, 1, stride=1, stride_axis=0)[:, :tq]

    row = lax.broadcasted_iota(jnp.int32, (ATTN_HEAD, w), 0)
    qts = []
    for lc in range(chunks):
        q = qt_ref[(lc * w) // TOKEN_TILE][:, (lc * w) % TOKEN_TILE:(lc * w) % TOKEN_TILE + w]
        zero = jnp.zeros_like(q)
        qts.append((jnp.where(row < HALF_DIM, q, zero), jnp.where(row >= HALF_DIM, q, zero)))

    lo = jnp.maximum(qi * r - 1, 0)
    hi = jnp.minimum((qi + 1) * r + 1, nkv)

    def logits(j, bias_view, c, lc):
        kj = k_ref[pl.ds(pl.multiple_of(j * tk, tk), tk), :]
        s = jnp.dot(kj, qts[lc][c], preferred_element_type=F32)
        if bias_view is not None:
            s = jnp.maximum(s, NEG_BIG) + bias_view[:, lc * w:(lc + 1) * w]
        return s

    def pv(j, p):
        pb = p.astype(BF16)
        out = jnp.dot(vt_ref[j * pieces], pb[:TOKEN_TILE], preferred_element_type=F32)
        for i in range(1, pieces):
            out += jnp.dot(vt_ref[j * pieces + i], pb[i * TOKEN_TILE:(i + 1) * TOKEN_TILE],
                           preferred_element_type=F32)
        return out

    def safe_block(j, bias_view, c_row):
        ss = [[logits(j, bias_view, c, lc) for lc in range(chunks)] for c in range(2)]
        for c in range(2):
            for lc in range(chunks):
                sl = slice(lc * w, (lc + 1) * w)
                s = ss[c][lc]
                m_old = m_sc[c, :, sl]
                if bias_view is not None:
                    m_new = jnp.maximum(m_old, jnp.max(s, axis=0, keepdims=True))
                    p = jnp.exp2(s - m_new)
                else:
                    cr = c_row[:, sl]
                    m_new = jnp.maximum(m_old, jnp.max(s, axis=0, keepdims=True) + cr)
                    p = jnp.exp2(s - (m_new - cr))
                alpha = jnp.exp2(m_old - m_new)
                l_sc[c, :, sl] = alpha * l_sc[c, :, sl] + jnp.sum(p, axis=0, keepdims=True)
                acc_sc[c, :, sl] = alpha * acc_sc[c, :, sl] + pv(j, p)
                m_sc[c, :, sl] = m_new

    def fast_block(j, bias_view, c_row):
        ss = [[logits(j, None, c, lc) for lc in range(chunks)] for c in range(2)]
        for c in range(2):
            for lc in range(chunks):
                sl = slice(lc * w, (lc + 1) * w)
                s = ss[c][lc]
                m_old = m_sc[c, :, sl]
                if bias_view is not None:
                    e = (s - m_old) + bias_view[:, sl]
                    p = jnp.exp2(e)
                    bm = jnp.max(e, axis=0, keepdims=True) + m_old
                else:
                    cr = c_row[:, sl]
                    p = jnp.exp2(s - (m_old - cr))
                    bm = jnp.max(s, axis=0, keepdims=True) + cr
                m_new = jnp.maximum(m_old, bm)
                alpha = jnp.exp2(m_old - m_new)
                ex_sc[c, :, sl] = jnp.maximum(ex_sc[c, :, sl], bm - m_old)
                l_sc[c, :, sl] = alpha * (l_sc[c, :, sl] + jnp.sum(p, axis=0, keepdims=True))
                acc_sc[c, :, sl] = alpha * (acc_sc[c, :, sl] + pv(j, p))
                m_sc[c, :, sl] = m_new

    def run(block_fn, pair_far):
        def far(c_row):
            def body(j, carry):
                block_fn(j, None, c_row)
                return carry
            return body

        def near(j, carry):
            block_fn(j, bias_sc.at[j - (qi * r - 1)], None)
            return carry

        def far_pair(c_row):
            def body(i, first):
                block_fn(first + 2 * i, None, c_row)
                block_fn(first + 2 * i + 1, None, c_row)
                return first
            return body

        if pair_far:
            n_lo = lo // 2
            lax.fori_loop(0, n_lo, far_pair(cfar_ref[0]), 0)
            lax.fori_loop(2 * n_lo, lo, far(cfar_ref[0]), 0)
            lax.fori_loop(lo, hi, near, 0)
            n_hi = (nkv - hi) // 2
            lax.fori_loop(0, n_hi, far_pair(cfar_ref[1]), hi)
            lax.fori_loop(hi + 2 * n_hi, nkv, far(cfar_ref[1]), 0)
        else:
            lax.fori_loop(0, lo, far(cfar_ref[0]), 0)
            lax.fori_loop(lo, hi, near, 0)
            lax.fori_loop(hi, nkv, far(cfar_ref[1]), 0)

    l_sc[...] = jnp.zeros(l_sc.shape, F32)
    acc_sc[...] = jnp.zeros(acc_sc.shape, F32)
    ex_sc[...] = jnp.zeros(ex_sc.shape, F32)

    k0 = k_ref[0:ATTN_LEVEL_ROWS, :]

    @pl.when(qi == 0)
    def _():
        for c in range(2):
            for lc in range(chunks):
                sl = slice(lc * w, (lc + 1) * w)
                s0 = jnp.dot(k0, qts[lc][c], preferred_element_type=F32)
                s0 = jnp.maximum(s0, NEG_BIG) + bias_sc[1, 0:ATTN_LEVEL_ROWS, sl]
                m_sc[c, :, sl] = jnp.max(s0, axis=0, keepdims=True)

    @pl.when(qi != 0)
    def _():
        for c in range(2):
            for lc in range(chunks):
                sl = slice(lc * w, (lc + 1) * w)
                s0 = jnp.dot(k0, qts[lc][c], preferred_element_type=F32)
                m_sc[c, :, sl] = jnp.max(s0, axis=0, keepdims=True) + cfar_ref[0][:, sl]

    run(fast_block, True)

    @pl.when(jnp.max(ex_sc[...]) > ATTN_EXP_GUARD)
    def _():
        m_sc[...] = jnp.full(m_sc.shape, NEG_BIG, F32)
        l_sc[...] = jnp.zeros(l_sc.shape, F32)
        acc_sc[...] = jnp.zeros(acc_sc.shape, F32)
        run(safe_block, False)

    lam = (jnp.exp(jnp.sum(lam_ref[0:1, :] * lam_ref[1:2, :], axis=-1, keepdims=True))
           - jnp.exp(jnp.sum(lam_ref[2:3, :] * lam_ref[3:4, :], axis=-1, keepdims=True))
           + LAM_INIT)
    o = acc_sc[0] / l_sc[0] - lam * (acc_sc[1] / l_sc[1])
    ms = jnp.mean(o * o, axis=0, keepdims=True)
    o = o * lax.rsqrt(ms + EPS) * g_ref[...]
    o = o * (1.0 - LAM_INIT)
    o_ref[...] = o.T.astype(BF16)


def _attention(qt, k, vt, wtab, cfar, lam_params, subln_g, b, s):
    tq, tk = ATTN_Q_TILE, ATTN_KV_TILE
    r = tq // tk
    nq = s // tq
    nkv = s // tk
    t = b * s
    return pl.pallas_call(
        functools.partial(_attn_kernel, nkv=nkv, tq=tq, tk=tk),
        grid=(N_HEADS, b, nq),
        in_specs=[
            pl.BlockSpec((4, HALF_DIM), lambda h, bi, qi: (0, 0)),
            pl.BlockSpec((tq // TOKEN_TILE, ATTN_HEAD, TOKEN_TILE), lambda h, bi, qi: (bi * nq + qi, h, 0)),
            pl.BlockSpec((s, ATTN_HEAD), lambda h, bi, qi: (bi, h)),
            pl.BlockSpec((s // TOKEN_TILE, ATTN_HEAD, TOKEN_TILE), lambda h, bi, qi: (bi, h, 0)),
            pl.BlockSpec((None, r + 2, 1, tq + tk), lambda h, bi, qi: (h, 0, 0, 0)),
            pl.BlockSpec((None, 2, 1, tq), lambda h, bi, qi: (h, 0, 0, 0)),
            pl.BlockSpec((ATTN_HEAD, 1), lambda h, bi, qi: (0, 0)),
        ],
        out_specs=pl.BlockSpec((tq, ATTN_HEAD), lambda h, bi, qi: (bi * nq + qi, h)),
        out_shape=jax.ShapeDtypeStruct((t, ATTN_WIDTH), BF16),
        scratch_shapes=[
            pltpu.VMEM((2, 1, tq), F32),
            pltpu.VMEM((2, 1, tq), F32),
            pltpu.VMEM((2, ATTN_HEAD, tq), F32),
            pltpu.VMEM((r + 2, tk, tq), F32),
            pltpu.VMEM((2, 1, tq), F32),
        ],
        compiler_params=_cparams(("arbitrary", "arbitrary", "arbitrary")),
        name="diff_attention",
    )(lam_params, qt, k, vt, wtab, cfar, subln_g.reshape(ATTN_HEAD, 1))


def _outproj_kernel(x_ref, yf_ref, ya_ref, w_ref, g_ref, x1_ref, h2_ref):
    y = (jnp.dot(yf_ref[...], w_ref[:FOURIER_WIDTH, :], preferred_element_type=F32)
         + jnp.dot(ya_ref[...], w_ref[FOURIER_WIDTH:, :], preferred_element_type=F32))
    x1 = x_ref[...] + y
    x1_ref[...] = x1
    ms = jnp.mean(x1 * x1, axis=-1, keepdims=True)
    h2_ref[...] = (x1 * lax.rsqrt(ms + EPS) * g_ref[...]).astype(BF16)


def _outproj(x2d, yf, ya, w_out_bf, g2):
    t = x2d.shape[0]
    tm = TOKEN_TILE
    return pl.pallas_call(
        _outproj_kernel,
        grid=(t // tm,),
        in_specs=[
            pl.BlockSpec((tm, D_MODEL), lambda i: (i, 0)),
            pl.BlockSpec((tm, FOURIER_WIDTH), lambda i: (i, 0)),
            pl.BlockSpec((tm, ATTN_WIDTH), lambda i: (i, 0)),
            pl.BlockSpec((D_MODEL, D_MODEL), lambda i: (0, 0)),
            pl.BlockSpec((1, D_MODEL), lambda i: (0, 0)),
        ],
        out_specs=[
            pl.BlockSpec((tm, D_MODEL), lambda i: (i, 0)),
            pl.BlockSpec((tm, D_MODEL), lambda i: (i, 0)),
        ],
        out_shape=[
            jax.ShapeDtypeStruct((t, D_MODEL), F32),
            jax.ShapeDtypeStruct((t, D_MODEL), BF16),
        ],
        compiler_params=_cparams(("arbitrary",)),
        name="outproj",
    )(x2d, yf, ya, w_out_bf, g2.reshape(1, D_MODEL))


def _ffn_kernel(h_ref, x1_ref, wg_ref, wu_ref, wd_ref, g_ref, o_ref, acc_ref):
    f = pl.program_id(1)

    @pl.when(f == 0)
    def _():
        acc_ref[...] = jnp.zeros(acc_ref.shape, F32)

    h = h_ref[...]
    gate = jnp.dot(h, wg_ref[...], preferred_element_type=F32)
    up = jnp.dot(h, wu_ref[...], preferred_element_type=F32)
    a = (gate * (1.0 / (1.0 + jnp.exp(-gate))) * up).astype(BF16)
    acc_ref[...] += jnp.dot(a, wd_ref[...], preferred_element_type=F32)

    @pl.when(f == pl.num_programs(1) - 1)
    def _():
        x2 = x1_ref[...] + acc_ref[...]
        ms = jnp.mean(x2 * x2, axis=-1, keepdims=True)
        o_ref[...] = x2 * lax.rsqrt(ms + EPS) * g_ref[...]


def _ffn(h2, x1, wg_bf, wu_bf, wd_bf, gf):
    t = h2.shape[0]
    tm = TOKEN_TILE
    tf = FF_TILE
    return pl.pallas_call(
        _ffn_kernel,
        grid=(t // tm, D_FF // tf),
        in_specs=[
            pl.BlockSpec((tm, D_MODEL), lambda i, f: (i, 0)),
            pl.BlockSpec((tm, D_MODEL), lambda i, f: (i, 0)),
            pl.BlockSpec((D_MODEL, tf), lambda i, f: (0, f)),
            pl.BlockSpec((D_MODEL, tf), lambda i, f: (0, f)),
            pl.BlockSpec((tf, D_MODEL), lambda i, f: (f, 0)),
            pl.BlockSpec((1, D_MODEL), lambda i, f: (0, 0)),
        ],
        out_specs=pl.BlockSpec((tm, D_MODEL), lambda i, f: (i, 0)),
        out_shape=jax.ShapeDtypeStruct((t, D_MODEL), F32),
        scratch_shapes=[pltpu.VMEM((tm, D_MODEL), F32)],
        compiler_params=_cparams(("arbitrary", "arbitrary")),
        name="ffn",
    )(h2, x1, wg_bf, wu_bf, wd_bf, gf.reshape(1, D_MODEL))


def _encoder(x, params):
    b, s, _ = x.shape
    t = b * s
    n1 = s // SEQ_MINOR
    x2d = x.reshape(t, D_MODEL)

    pq, qt, k, vt = _inproj(x2d, params["norm1_g"], params["w_in"], params["ab"])
    y1 = _fourier1(pq.reshape(2, b, n1, SEQ_MINOR, FOURIER_WIDTH), n1)
    yf = _fourier2(y1, n1).reshape(t, FOURIER_WIDTH)
    ya = _attention(qt, k, vt, params["wtab"], params["cfar"], params["lam"], params["subln_g"], b, s)
    x1, h2 = _outproj(x2d, yf, ya, params["w_out"], params["norm2_g"])
    out = _ffn(h2, x1, params["w_gate"], params["w_up"], params["w_down"], params["final_g"])
    return out.reshape(b, s, D_MODEL)


def kernel(x_prompt, x_sample, norm1_g, w_in, w_fourier, lambda_q1, lambda_k1, lambda_q2, lambda_k2,
           subln_g, w_out, norm2_g, w_gate, w_up, w_down, rel_bias, final_g):
    wtab, cfar = _bias_tables(rel_bias)
    params = {
        "norm1_g": norm1_g[0].astype(F32),
        "w_in": w_in[0].astype(BF16),
        "ab": _fourier_weights(w_fourier[0].astype(F32)),
        "lam": jnp.stack([lambda_q1[0], lambda_k1[0], lambda_q2[0], lambda_k2[0]]).astype(F32),
        "subln_g": subln_g[0].astype(F32),
        "w_out": w_out[0].astype(BF16),
        "norm2_g": norm2_g[0].astype(F32),
        "w_gate": w_gate[0].astype(BF16),
        "w_up": w_up[0].astype(BF16),
        "w_down": w_down[0].astype(BF16),
        "wtab": wtab,
        "cfar": cfar,
        "final_g": final_g.astype(F32),
    }
    return (_encoder(x_prompt, params), _encoder(x_sample, params))
```

```python
import functools
import math

import numpy as np
import jax
import jax.numpy as jnp
from jax import lax
from jax.experimental import pallas as pl
from jax.experimental.pallas import tpu as pltpu

F32 = jnp.float32
BF16 = jnp.bfloat16

D_MODEL = 2048
FOURIER_WIDTH = 1024
FOURIER_GROUP = 128
N_FGROUPS = 8
ATTN_WIDTH = 1024
ATTN_HEAD = 128
N_HEADS = 8
HALF_DIM = 64
D_FF = 5632
NUM_BUCKETS = 32
MAX_DISTANCE = 128
EPS = 1e-6
LAM_INIT = 0.8 - 0.6 * math.exp(-0.3 * 0)

SEQ_MINOR = 128
TOKEN_TILE = 512
FF_TILE = 512
ATTN_Q_TILE = 1024
ATTN_KV_TILE = 1024
ATTN_EXP_GUARD = 64.0
ATTN_LEVEL_ROWS = 64
ATTN_LANE_CHUNK = 256
LOG2E = math.log2(math.e)
NEG_BIG = -1e30
VMEM_LIMIT = 56 * 1024 * 1024


def _cparams(sem):
    return pltpu.CompilerParams(dimension_semantics=sem, vmem_limit_bytes=VMEM_LIMIT)


def _dft128_tables():
    n = np.arange(FOURIER_GROUP)
    ang = 2.0 * np.pi * ((n[:, None] * n[None, :]) % FOURIER_GROUP) / FOURIER_GROUP
    return np.stack([np.cos(ang), np.sin(ang)]).astype(np.float32)


def _stage1_matrix(n1):
    n = np.arange(n1)
    ang = 2.0 * np.pi * ((n[:, None] * n[None, :]) % n1) / n1
    c, s = np.cos(ang), np.sin(ang)
    return np.block([[c, -s], [-s, -c]]).astype(np.float32)


def _stage2_matrices(n1):
    s_len = n1 * SEQ_MINOR
    s1p = np.arange(n1)[:, None, None]
    s2p = np.arange(SEQ_MINOR)[None, :, None]
    s2 = np.arange(SEQ_MINOR)[None, None, :]
    ang = 2.0 * np.pi * ((s2 * (s1p + n1 * s2p)) % s_len) / s_len
    scale = 1.0 / math.sqrt(s_len * FOURIER_GROUP)
    return (np.concatenate([np.cos(ang), np.sin(ang)], axis=-1) * scale).astype(np.float32)


def _fourier_weight_kernel(dft_ref, w_ref, ab_ref):
    for g in range(N_FGROUPS):
        w = w_ref[g]
        a = jnp.dot(dft_ref[0], w, preferred_element_type=F32, precision=lax.Precision.HIGHEST)
        b = jnp.dot(dft_ref[1], w, preferred_element_type=F32, precision=lax.Precision.HIGHEST)
        ab_ref[g, :, :FOURIER_GROUP] = a.astype(BF16)
        ab_ref[g, :, FOURIER_GROUP:] = b.astype(BF16)


def _fourier_weights(w_f):
    return pl.pallas_call(
        _fourier_weight_kernel,
        out_shape=jax.ShapeDtypeStruct((N_FGROUPS, FOURIER_GROUP, 2 * FOURIER_GROUP), BF16),
        name="fourier_weights",
    )(jnp.asarray(_dft128_tables()), w_f)


def _inproj_kernel(x_ref, g_ref, w_ref, ab_ref, pq_ref, qt_ref, k_ref, vt_ref):
    x = x_ref[...]
    ms = jnp.mean(x * x, axis=-1, keepdims=True)
    h = (x * lax.rsqrt(ms + EPS) * g_ref[...]).astype(BF16)

    def proj(col):
        return jnp.dot(h, w_ref[:, col:col + ATTN_WIDTH], preferred_element_type=F32)

    ub = proj(0).astype(BF16)
    for g in range(N_FGROUPS):
        lo, hi = g * FOURIER_GROUP, (g + 1) * FOURIER_GROUP
        pq = jnp.dot(ub[:, lo:hi], ab_ref[g], preferred_element_type=F32)
        pq_ref[0, :, lo:hi] = pq[:, :FOURIER_GROUP].astype(BF16)
        pq_ref[1, :, lo:hi] = pq[:, FOURIER_GROUP:].astype(BF16)
    qt_ref[0] = (proj(FOURIER_WIDTH) * (HALF_DIM ** -0.5 * LOG2E)).T.astype(BF16)
    k_ref[...] = proj(FOURIER_WIDTH + ATTN_WIDTH).astype(BF16)
    vt_ref[0] = proj(FOURIER_WIDTH + 2 * ATTN_WIDTH).T.astype(BF16)


def _inproj(x2d, g, w_in_bf, ab):
    t = x2d.shape[0]
    tm = TOKEN_TILE
    nt = t // tm
    return pl.pallas_call(
        _inproj_kernel,
        grid=(nt,),
        in_specs=[
            pl.BlockSpec((tm, D_MODEL), lambda i: (i, 0)),
            pl.BlockSpec((1, D_MODEL), lambda i: (0, 0)),
            pl.BlockSpec((D_MODEL, FOURIER_WIDTH + 3 * ATTN_WIDTH), lambda i: (0, 0), pipeline_mode=pl.Buffered(1)),
            pl.BlockSpec((N_FGROUPS, FOURIER_GROUP, 2 * FOURIER_GROUP), lambda i: (0, 0, 0)),
        ],
        out_specs=[
            pl.BlockSpec((2, tm, FOURIER_WIDTH), lambda i: (0, i, 0)),
            pl.BlockSpec((1, ATTN_WIDTH, tm), lambda i: (i, 0, 0)),
            pl.BlockSpec((tm, ATTN_WIDTH), lambda i: (i, 0)),
            pl.BlockSpec((1, ATTN_WIDTH, tm), lambda i: (i, 0, 0)),
        ],
        out_shape=[
            jax.ShapeDtypeStruct((2, t, FOURIER_WIDTH), BF16),
            jax.ShapeDtypeStruct((nt, ATTN_WIDTH, tm), BF16),
            jax.ShapeDtypeStruct((t, ATTN_WIDTH), BF16),
            jax.ShapeDtypeStruct((nt, ATTN_WIDTH, tm), BF16),
        ],
        compiler_params=_cparams(("arbitrary",)),
        name="inproj",
    )(x2d, g.reshape(1, D_MODEL), w_in_bf, ab)


def _fourier1_kernel(x_ref, d_ref, o_ref, *, n1, rows):
    x = pltpu.einshape("abc->bac", x_ref[...].reshape(2 * n1, rows, FOURIER_WIDTH))
    d = d_ref[...]
    for r in range(rows):
        y = jnp.dot(d, x[r], preferred_element_type=F32)
        o_ref[0, r] = y[:n1].astype(BF16)
        o_ref[1, r] = y[n1:].astype(BF16)


def _fourier1(pq5, n1):
    b = pq5.shape[1]
    rows = 16 if n1 >= 64 else 64
    d1 = jnp.asarray(_stage1_matrix(n1)).astype(BF16)
    return pl.pallas_call(
        functools.partial(_fourier1_kernel, n1=n1, rows=rows),
        grid=(b, SEQ_MINOR // rows),
        in_specs=[
            pl.BlockSpec((2, None, n1, rows, FOURIER_WIDTH), lambda i, j: (0, i, 0, j, 0)),
            pl.BlockSpec((2 * n1, 2 * n1), lambda i, j: (0, 0)),
        ],
        out_specs=pl.BlockSpec((None, 2, rows, n1, FOURIER_WIDTH), lambda i, j: (i, 0, j, 0, 0)),
        out_shape=jax.ShapeDtypeStruct((b, 2, SEQ_MINOR, n1, FOURIER_WIDTH), BF16),
        compiler_params=_cparams(("arbitrary", "arbitrary")),
        name="fourier_stage1",
    )(pq5, d1)


def _fourier2_kernel(x_ref, m_ref, o_ref, *, rows):
    x = pltpu.einshape("abc->bac", x_ref[...].reshape(2 * SEQ_MINOR, rows, FOURIER_WIDTH))
    ys = [jnp.dot(m_ref[i], x[i], preferred_element_type=F32).astype(BF16) for i in range(rows)]
    o_ref[...] = pltpu.einshape("abc->bac", jnp.stack(ys))


def _fourier2(y1, n1):
    b = y1.shape[0]
    rows = 16
    m2 = jnp.asarray(_stage2_matrices(n1)).astype(BF16)
    return pl.pallas_call(
        functools.partial(_fourier2_kernel, rows=rows),
        grid=(b, n1 // rows),
        in_specs=[
            pl.BlockSpec((None, 2, SEQ_MINOR, rows, FOURIER_WIDTH), lambda i, j: (i, 0, 0, j, 0)),
            pl.BlockSpec((rows, SEQ_MINOR, 2 * SEQ_MINOR), lambda i, j: (j, 0, 0)),
        ],
        out_specs=pl.BlockSpec((None, SEQ_MINOR, rows, FOURIER_WIDTH), lambda i, j: (i, 0, j, 0)),
        out_shape=jax.ShapeDtypeStruct((b, SEQ_MINOR, n1, FOURIER_WIDTH), BF16),
        compiler_params=_cparams(("arbitrary", "arbitrary")),
        name="fourier_stage2",
    )(y1, m2)


def _t5_bucket(rel):
    nb = NUM_BUCKETS // 2
    max_exact = nb // 2
    ret = (rel > 0).astype(jnp.int32) * nb
    n = jnp.abs(rel)
    nf = jnp.maximum(n, 1).astype(jnp.float32)
    large = max_exact + (jnp.log(nf / max_exact) / math.log(MAX_DISTANCE / max_exact)
                         * (nb - max_exact)).astype(jnp.int32)
    large = jnp.minimum(large, nb - 1)
    return ret + jnp.where(n < max_exact, n, large)


def _bias_tables(rel_bias):
    tq, tk = ATTN_Q_TILE, ATTN_KV_TILE
    r = tq // tk
    wd = tq + tk
    mp = np.arange(wd)
    m = np.where(mp < tq, mp, mp - wd)
    d = (np.arange(r + 2)[:, None] - 1) * tk - m[None, :]
    tbl = rel_bias.astype(F32) * LOG2E
    w = tbl[_t5_bucket(jnp.asarray(d, jnp.int32))]
    wtab = jnp.transpose(w, (2, 0, 1))[:, :, None, :]
    far = tbl[_t5_bucket(jnp.asarray([-2 * tk, 2 * tk], jnp.int32))]
    cfar = jnp.broadcast_to(far.T[:, :, None, None], (N_HEADS, 2, 1, tq))
    return wtab, cfar


def _attn_kernel(lam_ref, qt_ref, k_ref, vt_ref, w_ref, cfar_ref, g_ref, o_ref, m_sc, l_sc, acc_sc, bias_sc, ex_sc,
                 *, nkv, tq, tk):
    qi = pl.program_id(2)
    r = tq // tk
    w = ATTN_LANE_CHUNK
    chunks = tq // w
    pieces = tk // TOKEN_TILE

    @pl.when((pl.program_id(1) == 0) & (qi == 0))
    def _():
        for t in range(r + 2):
            x = jnp.broadcast_to(w_ref[t], (tk, tq + tk))
            bias_sc[t] = pltpu.roll(x, 0, 1, stride=1, stride_axis=0)[:, :tq]

    row = lax.broadcasted_iota(jnp.int32, (ATTN_HEAD, w), 0)
    qts = []
    for lc in range(chunks):
        q = qt_ref[(lc * w) // TOKEN_TILE][:, (lc * w) % TOKEN_TILE:(lc * w) % TOKEN_TILE + w]
        zero = jnp.zeros_like(q)
        qts.append((jnp.where(row < HALF_DIM, q, zero), jnp.where(row >= HALF_DIM, q, zero)))

    lo = jnp.maximum(qi * r - 1, 0)
    hi = jnp.minimum((qi + 1) * r + 1, nkv)

    def logits(j, bias_view, c, lc):
        kj = k_ref[pl.ds(pl.multiple_of(j * tk, tk), tk), :]
        s = jnp.dot(kj, qts[lc][c], preferred_element_type=F32)
        if bias_view is not None:
            s = jnp.maximum(s, NEG_BIG) + bias_view[:, lc * w:(lc + 1) * w]
        return s

    def pv(j, p):
        pb = p.astype(BF16)
        out = jnp.dot(vt_ref[j * pieces], pb[:TOKEN_TILE], preferred_element_type=F32)
        for i in range(1, pieces):
            out += jnp.dot(vt_ref[j * pieces + i], pb[i * TOKEN_TILE:(i + 1) * TOKEN_TILE],
                           preferred_element_type=F32)
        return out

    def safe_block(j, bias_view, c_row):
        ss = [[logits(j, bias_view, c, lc) for lc in range(chunks)] for c in range(2)]
        for c in range(2):
            for lc in range(chunks):
                sl = slice(lc * w, (lc + 1) * w)
                s = ss[c][lc]
                m_old = m_sc[c, :, sl]
                if bias_view is not None:
                    m_new = jnp.maximum(m_old, jnp.max(s, axis=0, keepdims=True))
                    p = jnp.exp2(s - m_new)
                else:
                    cr = c_row[:, sl]
                    m_new = jnp.maximum(m_old, jnp.max(s, axis=0, keepdims=True) + cr)
                    p = jnp.exp2(s - (m_new - cr))
                alpha = jnp.exp2(m_old - m_new)
                l_sc[c, :, sl] = alpha * l_sc[c, :, sl] + jnp.sum(p, axis=0, keepdims=True)
                acc_sc[c, :, sl] = alpha * acc_sc[c, :, sl] + pv(j, p)
                m_sc[c, :, sl] = m_new

    def fast_block(j, tile, c_row):
        ss = [[logits(j, None, c, lc) for lc in range(chunks)] for c in range(2)]
        for c in range(2):
            for lc in range(chunks):
                sl = slice(lc * w, (lc + 1) * w)
                s = ss[c][lc]
                m_old = m_sc[c, :, sl]
                if tile is not None:
                    es = []
                    for a in range(tk // w):
                        rows = slice(a * w, (a + 1) * w)
                        dist = (tile - 1) * tk + (a - lc) * w
                        if abs(dist) >= w - 1 + MAX_DISTANCE:
                            side = cfar_ref[0 if dist < 0 else 1][:, sl]
                            es.append(s[rows] - (m_old - side))
                        else:
                            es.append((s[rows] - m_old) + bias_sc[tile, rows, sl])
                    e = jnp.concatenate(es, axis=0)
                    p = jnp.exp2(e)
                    bm = jnp.max(e, axis=0, keepdims=True) + m_old
                else:
                    cr = c_row[:, sl]
                    p = jnp.exp2(s - (m_old - cr))
                    bm = jnp.max(s, axis=0, keepdims=True) + cr
                m_new = jnp.maximum(m_old, bm)
                alpha = jnp.exp2(m_old - m_new)
                ex_sc[c, :, sl] = jnp.maximum(ex_sc[c, :, sl], bm - m_old)
                l_sc[c, :, sl] = alpha * (l_sc[c, :, sl] + jnp.sum(p, axis=0, keepdims=True))
                acc_sc[c, :, sl] = alpha * (acc_sc[c, :, sl] + pv(j, p))
                m_sc[c, :, sl] = m_new

    def run(block_fn, pair_far):
        def far(c_row):
            def body(j, carry):
                block_fn(j, None, c_row)
                return carry
            return body

        def near(j, carry):
            block_fn(j, bias_sc.at[j - (qi * r - 1)], None)
            return carry

        def far_pair(c_row):
            def body(i, first):
                block_fn(first + 2 * i, None, c_row)
                block_fn(first + 2 * i + 1, None, c_row)
                return first
            return body

        if pair_far:
            n_lo = lo // 2
            lax.fori_loop(0, n_lo, far_pair(cfar_ref[0]), 0)
            lax.fori_loop(2 * n_lo, lo, far(cfar_ref[0]), 0)
            for t in range(r + 2):
                j_t = qi * r - 1 + t

                @pl.when((j_t >= 0) & (j_t < nkv))
                def _():
                    block_fn(j_t, t, None)
            n_hi = (nkv - hi) // 2
            lax.fori_loop(0, n_hi, far_pair(cfar_ref[1]), hi)
            lax.fori_loop(hi + 2 * n_hi, nkv, far(cfar_ref[1]), 0)
        else:
            lax.fori_loop(0, lo, far(cfar_ref[0]), 0)
            lax.fori_loop(lo, hi, near, 0)
            lax.fori_loop(hi, nkv, far(cfar_ref[1]), 0)

    l_sc[...] = jnp.zeros(l_sc.shape, F32)
    acc_sc[...] = jnp.zeros(acc_sc.shape, F32)
    ex_sc[...] = jnp.zeros(ex_sc.shape, F32)

    k0 = k_ref[0:ATTN_LEVEL_ROWS, :]

    @pl.when(qi == 0)
    def _():
        for c in range(2):
            for lc in range(chunks):
                sl = slice(lc * w, (lc + 1) * w)
                s0 = jnp.dot(k0, qts[lc][c], preferred_element_type=F32)
                s0 = jnp.maximum(s0, NEG_BIG) + bias_sc[1, 0:ATTN_LEVEL_ROWS, sl]
                m_sc[c, :, sl] = jnp.max(s0, axis=0, keepdims=True)

    @pl.when(qi != 0)
    def _():
        for c in range(2):
            for lc in range(chunks):
                sl = slice(lc * w, (lc + 1) * w)
                s0 = jnp.dot(k0, qts[lc][c], preferred_element_type=F32)
                m_sc[c, :, sl] = jnp.max(s0, axis=0, keepdims=True) + cfar_ref[0][:, sl]

    run(fast_block, True)

    @pl.when(jnp.max(ex_sc[...]) > ATTN_EXP_GUARD)
    def _():
        m_sc[...] = jnp.full(m_sc.shape, NEG_BIG, F32)
        l_sc[...] = jnp.zeros(l_sc.shape, F32)
        acc_sc[...] = jnp.zeros(acc_sc.shape, F32)
        run(safe_block, False)

    lam = (jnp.exp(jnp.sum(lam_ref[0:1, :] * lam_ref[1:2, :], axis=-1, keepdims=True))
           - jnp.exp(jnp.sum(lam_ref[2:3, :] * lam_ref[3:4, :], axis=-1, keepdims=True))
           + LAM_INIT)
    o = acc_sc[0] / l_sc[0] - lam * (acc_sc[1] / l_sc[1])
    ms = jnp.mean(o * o, axis=0, keepdims=True)
    o = o * lax.rsqrt(ms + EPS) * g_ref[...]
    o = o * (1.0 - LAM_INIT)
    o_ref[...] = o.T.astype(BF16)


def _attention(qt, k, vt, wtab, cfar, lam_params, subln_g, b, s):
    tq, tk = ATTN_Q_TILE, ATTN_KV_TILE
    r = tq // tk
    nq = s // tq
    nkv = s // tk
    t = b * s
    return pl.pallas_call(
        functools.partial(_attn_kernel, nkv=nkv, tq=tq, tk=tk),
        grid=(N_HEADS, b, nq),
        in_specs=[
            pl.BlockSpec((4, HALF_DIM), lambda h, bi, qi: (0, 0)),
            pl.BlockSpec((tq // TOKEN_TILE, ATTN_HEAD, TOKEN_TILE), lambda h, bi, qi: (bi * nq + qi, h, 0)),
            pl.BlockSpec((s, ATTN_HEAD), lambda h, bi, qi: (bi, h)),
            pl.BlockSpec((s // TOKEN_TILE, ATTN_HEAD, TOKEN_TILE), lambda h, bi, qi: (bi, h, 0)),
            pl.BlockSpec((None, r + 2, 1, tq + tk), lambda h, bi, qi: (h, 0, 0, 0)),
            pl.BlockSpec((None, 2, 1, tq), lambda h, bi, qi: (h, 0, 0, 0)),
            pl.BlockSpec((ATTN_HEAD, 1), lambda h, bi, qi: (0, 0)),
        ],
        out_specs=pl.BlockSpec((tq, ATTN_HEAD), lambda h, bi, qi: (bi * nq + qi, h)),
        out_shape=jax.ShapeDtypeStruct((t, ATTN_WIDTH), BF16),
        scratch_shapes=[
            pltpu.VMEM((2, 1, tq), F32),
            pltpu.VMEM((2, 1, tq), F32),
            pltpu.VMEM((2, ATTN_HEAD, tq), F32),
            pltpu.VMEM((r + 2, tk, tq), F32),
            pltpu.VMEM((2, 1, tq), F32),
        ],
        compiler_params=_cparams(("arbitrary", "arbitrary", "arbitrary")),
        name="diff_attention",
    )(lam_params, qt, k, vt, wtab, cfar, subln_g.reshape(ATTN_HEAD, 1))


def _outproj_kernel(x_ref, yf_ref, ya_ref, w_ref, g_ref, x1_ref, h2_ref):
    y = (jnp.dot(yf_ref[...], w_ref[:FOURIER_WIDTH, :], preferred_element_type=F32)
         + jnp.dot(ya_ref[...], w_ref[FOURIER_WIDTH:, :], preferred_element_type=F32))
    x1 = x_ref[...] + y
    x1_ref[...] = x1
    ms = jnp.mean(x1 * x1, axis=-1, keepdims=True)
    h2_ref[...] = (x1 * lax.rsqrt(ms + EPS) * g_ref[...]).astype(BF16)


def _outproj(x2d, yf, ya, w_out_bf, g2):
    t = x2d.shape[0]
    tm = TOKEN_TILE
    return pl.pallas_call(
        _outproj_kernel,
        grid=(t // tm,),
        in_specs=[
            pl.BlockSpec((tm, D_MODEL), lambda i: (i, 0)),
            pl.BlockSpec((tm, FOURIER_WIDTH), lambda i: (i, 0)),
            pl.BlockSpec((tm, ATTN_WIDTH), lambda i: (i, 0)),
            pl.BlockSpec((D_MODEL, D_MODEL), lambda i: (0, 0)),
            pl.BlockSpec((1, D_MODEL), lambda i: (0, 0)),
        ],
        out_specs=[
            pl.BlockSpec((tm, D_MODEL), lambda i: (i, 0)),
            pl.BlockSpec((tm, D_MODEL), lambda i: (i, 0)),
        ],
        out_shape=[
            jax.ShapeDtypeStruct((t, D_MODEL), F32),
            jax.ShapeDtypeStruct((t, D_MODEL), BF16),
        ],
        compiler_params=_cparams(("arbitrary",)),
        name="outproj",
    )(x2d, yf, ya, w_out_bf, g2.reshape(1, D_MODEL))


def _ffn_kernel(h_ref, x1_ref, wg_ref, wu_ref, wd_ref, g_ref, o_ref, acc_ref):
    f = pl.program_id(1)

    @pl.when(f == 0)
    def _():
        acc_ref[...] = jnp.zeros(acc_ref.shape, F32)

    h = h_ref[...]
    gate = jnp.dot(h, wg_ref[...], preferred_element_type=F32)
    up = jnp.dot(h, wu_ref[...], preferred_element_type=F32)
    a = (gate * (1.0 / (1.0 + jnp.exp(-gate))) * up).astype(BF16)
    acc_ref[...] += jnp.dot(a, wd_ref[...], preferred_element_type=F32)

    @pl.when(f == pl.num_programs(1) - 1)
    def _():
        x2 = x1_ref[...] + acc_ref[...]
        ms = jnp.mean(x2 * x2, axis=-1, keepdims=True)
        o_ref[...] = x2 * lax.rsqrt(ms + EPS) * g_ref[...]


def _ffn(h2, x1, wg_bf, wu_bf, wd_bf, gf):
    t = h2.shape[0]
    tm = TOKEN_TILE
    tf = FF_TILE
    return pl.pallas_call(
        _ffn_kernel,
        grid=(t // tm, D_FF // tf),
        in_specs=[
            pl.BlockSpec((tm, D_MODEL), lambda i, f: (i, 0)),
            pl.BlockSpec((tm, D_MODEL), lambda i, f: (i, 0)),
            pl.BlockSpec((D_MODEL, tf), lambda i, f: (0, f)),
            pl.BlockSpec((D_MODEL, tf), lambda i, f: (0, f)),
            pl.BlockSpec((tf, D_MODEL), lambda i, f: (f, 0)),
            pl.BlockSpec((1, D_MODEL), lambda i, f: (0, 0)),
        ],
        out_specs=pl.BlockSpec((tm, D_MODEL), lambda i, f: (i, 0)),
        out_shape=jax.ShapeDtypeStruct((t, D_MODEL), F32),
        scratch_shapes=[pltpu.VMEM((tm, D_MODEL), F32)],
        compiler_params=_cparams(("arbitrary", "arbitrary")),
        name="ffn",
    )(h2, x1, wg_bf, wu_bf, wd_bf, gf.reshape(1, D_MODEL))


def _encoder(x, params):
    b, s, _ = x.shape
    t = b * s
    n1 = s // SEQ_MINOR
    x2d = x.reshape(t, D_MODEL)

    pq, qt, k, vt = _inproj(x2d, params["norm1_g"], params["w_in"], params["ab"])
    y1 = _fourier1(pq.reshape(2, b, n1, SEQ_MINOR, FOURIER_WIDTH), n1)
    yf = _fourier2(y1, n1).reshape(t, FOURIER_WIDTH)
    ya = _attention(qt, k, vt, params["wtab"], params["cfar"], params["lam"], params["subln_g"], b, s)
    x1, h2 = _outproj(x2d, yf, ya, params["w_out"], params["norm2_g"])
    out = _ffn(h2, x1, params["w_gate"], params["w_up"], params["w_down"], params["final_g"])
    return out.reshape(b, s, D_MODEL)


def kernel(x_prompt, x_sample, norm1_g, w_in, w_fourier, lambda_q1, lambda_k1, lambda_q2, lambda_k2,
           subln_g, w_out, norm2_g, w_gate, w_up, w_down, rel_bias, final_g):
    wtab, cfar = _bias_tables(rel_bias)
    params = {
        "norm1_g": norm1_g[0].astype(F32),
        "w_in": w_in[0].astype(BF16),
        "ab": _fourier_weights(w_fourier[0].astype(F32)),
        "lam": jnp.stack([lambda_q1[0], lambda_k1[0], lambda_q2[0], lambda_k2[0]]).astype(F32),
        "subln_g": subln_g[0].astype(F32),
        "w_out": w_out[0].astype(BF16),
        "norm2_g": norm2_g[0].astype(F32),
        "w_gate": w_gate[0].astype(BF16),
        "w_up": w_up[0].astype(BF16),
        "w_down": w_down[0].astype(BF16),
        "wtab": wtab,
        "cfar": cfar,
        "final_g": final_g.astype(F32),
    }
    return (_encoder(x_prompt, params), _encoder(x_sample, params))
```

```python
import functools
import math

import numpy as np
import jax
import jax.numpy as jnp
from jax import lax
from jax.experimental import pallas as pl
from jax.experimental.pallas import tpu as pltpu

F32 = jnp.float32
BF16 = jnp.bfloat16

D_MODEL = 2048
FOURIER_WIDTH = 1024
FOURIER_GROUP = 128
N_FGROUPS = 8
ATTN_WIDTH = 1024
ATTN_HEAD = 128
N_HEADS = 8
HALF_DIM = 64
D_FF = 5632
NUM_BUCKETS = 32
MAX_DISTANCE = 128
EPS = 1e-6
LAM_INIT = 0.8 - 0.6 * math.exp(-0.3 * 0)

SEQ_MINOR = 128
TOKEN_TILE = 512
FF_TILE = 512
ATTN_Q_TILE = 1024
ATTN_KV_TILE = 1024
ATTN_EXP_GUARD = 64.0
ATTN_LEVEL_ROWS = 16
ATTN_QK_AHEAD_NEAR = 4
ATTN_LANE_CHUNK = 256
LOG2E = math.log2(math.e)
NEG_BIG = -1e30
VMEM_LIMIT = 56 * 1024 * 1024


def _cparams(sem):
    return pltpu.CompilerParams(dimension_semantics=sem, vmem_limit_bytes=VMEM_LIMIT)


def _dft128_tables():
    n = np.arange(FOURIER_GROUP)
    ang = 2.0 * np.pi * ((n[:, None] * n[None, :]) % FOURIER_GROUP) / FOURIER_GROUP
    return np.stack([np.cos(ang), np.sin(ang)]).astype(np.float32)


def _stage1_matrix(n1):
    n = np.arange(n1)
    ang = 2.0 * np.pi * ((n[:, None] * n[None, :]) % n1) / n1
    c, s = np.cos(ang), np.sin(ang)
    return np.block([[c, -s], [-s, -c]]).astype(np.float32)


def _stage2_matrices(n1):
    s_len = n1 * SEQ_MINOR
    s1p = np.arange(n1)[:, None, None]
    s2p = np.arange(SEQ_MINOR)[None, :, None]
    s2 = np.arange(SEQ_MINOR)[None, None, :]
    ang = 2.0 * np.pi * ((s2 * (s1p + n1 * s2p)) % s_len) / s_len
    scale = 1.0 / math.sqrt(s_len * FOURIER_GROUP)
    return (np.concatenate([np.cos(ang), np.sin(ang)], axis=-1) * scale).astype(np.float32)


def _fourier_weight_kernel(dft_ref, w_ref, ab_ref):
    for g in range(N_FGROUPS):
        w = w_ref[g]
        a = jnp.dot(dft_ref[0], w, preferred_element_type=F32, precision=lax.Precision.HIGHEST)
        b = jnp.dot(dft_ref[1], w, preferred_element_type=F32, precision=lax.Precision.HIGHEST)
        ab_ref[g, :, :FOURIER_GROUP] = a.astype(BF16)
        ab_ref[g, :, FOURIER_GROUP:] = b.astype(BF16)


def _fourier_weights(w_f):
    return pl.pallas_call(
        _fourier_weight_kernel,
        out_shape=jax.ShapeDtypeStruct((N_FGROUPS, FOURIER_GROUP, 2 * FOURIER_GROUP), BF16),
        name="fourier_weights",
    )(jnp.asarray(_dft128_tables()), w_f)


def _inproj_kernel(x_ref, g_ref, w_ref, ab_ref, pq_ref, qt_ref, k_ref, vt_ref):
    x = x_ref[...]
    ms = jnp.mean(x * x, axis=-1, keepdims=True)
    h = (x * lax.rsqrt(ms + EPS) * g_ref[...]).astype(BF16)

    def proj(col):
        return jnp.dot(h, w_ref[:, col:col + ATTN_WIDTH], preferred_element_type=F32)

    ub = proj(0).astype(BF16)
    for g in range(N_FGROUPS):
        lo, hi = g * FOURIER_GROUP, (g + 1) * FOURIER_GROUP
        pq = jnp.dot(ub[:, lo:hi], ab_ref[g], preferred_element_type=F32)
        pq_ref[0, :, lo:hi] = pq[:, :FOURIER_GROUP].astype(BF16)
        pq_ref[1, :, lo:hi] = pq[:, FOURIER_GROUP:].astype(BF16)
    qt_ref[0] = (proj(FOURIER_WIDTH) * (HALF_DIM ** -0.5 * LOG2E)).T.astype(BF16)
    k_ref[...] = proj(FOURIER_WIDTH + ATTN_WIDTH).astype(BF16)
    vt_ref[0] = proj(FOURIER_WIDTH + 2 * ATTN_WIDTH).T.astype(BF16)


def _inproj(x2d, g, w_in_bf, ab):
    t = x2d.shape[0]
    tm = TOKEN_TILE
    nt = t // tm
    return pl.pallas_call(
        _inproj_kernel,
        grid=(nt,),
        in_specs=[
            pl.BlockSpec((tm, D_MODEL), lambda i: (i, 0)),
            pl.BlockSpec((1, D_MODEL), lambda i: (0, 0)),
            pl.BlockSpec((D_MODEL, FOURIER_WIDTH + 3 * ATTN_WIDTH), lambda i: (0, 0), pipeline_mode=pl.Buffered(1)),
            pl.BlockSpec((N_FGROUPS, FOURIER_GROUP, 2 * FOURIER_GROUP), lambda i: (0, 0, 0)),
        ],
        out_specs=[
            pl.BlockSpec((2, tm, FOURIER_WIDTH), lambda i: (0, i, 0)),
            pl.BlockSpec((1, ATTN_WIDTH, tm), lambda i: (i, 0, 0)),
            pl.BlockSpec((tm, ATTN_WIDTH), lambda i: (i, 0)),
            pl.BlockSpec((1, ATTN_WIDTH, tm), lambda i: (i, 0, 0)),
        ],
        out_shape=[
            jax.ShapeDtypeStruct((2, t, FOURIER_WIDTH), BF16),
            jax.ShapeDtypeStruct((nt, ATTN_WIDTH, tm), BF16),
            jax.ShapeDtypeStruct((t, ATTN_WIDTH), BF16),
            jax.ShapeDtypeStruct((nt, ATTN_WIDTH, tm), BF16),
        ],
        compiler_params=_cparams(("arbitrary",)),
        name="inproj",
    )(x2d, g.reshape(1, D_MODEL), w_in_bf, ab)


def _fourier1_kernel(x_ref, d_ref, o_ref, *, n1, rows):
    x = pltpu.einshape("abc->bac", x_ref[...].reshape(2 * n1, rows, FOURIER_WIDTH))
    d = d_ref[...]
    for r in range(rows):
        y = jnp.dot(d, x[r], preferred_element_type=F32)
        o_ref[0, r] = y[:n1].astype(BF16)
        o_ref[1, r] = y[n1:].astype(BF16)


def _fourier1(pq5, n1):
    b = pq5.shape[1]
    rows = 16 if n1 >= 64 else 64
    d1 = jnp.asarray(_stage1_matrix(n1)).astype(BF16)
    return pl.pallas_call(
        functools.partial(_fourier1_kernel, n1=n1, rows=rows),
        grid=(b, SEQ_MINOR // rows),
        in_specs=[
            pl.BlockSpec((2, None, n1, rows, FOURIER_WIDTH), lambda i, j: (0, i, 0, j, 0)),
            pl.BlockSpec((2 * n1, 2 * n1), lambda i, j: (0, 0)),
        ],
        out_specs=pl.BlockSpec((None, 2, rows, n1, FOURIER_WIDTH), lambda i, j: (i, 0, j, 0, 0)),
        out_shape=jax.ShapeDtypeStruct((b, 2, SEQ_MINOR, n1, FOURIER_WIDTH), BF16),
        compiler_params=_cparams(("arbitrary", "arbitrary")),
        name="fourier_stage1",
    )(pq5, d1)


def _fourier2_kernel(x_ref, m_ref, o_ref, *, rows):
    x = pltpu.einshape("abc->bac", x_ref[...].reshape(2 * SEQ_MINOR, rows, FOURIER_WIDTH))
    ys = [jnp.dot(m_ref[i], x[i], preferred_element_type=F32).astype(BF16) for i in range(rows)]
    o_ref[...] = pltpu.einshape("abc->bac", jnp.stack(ys))


def _fourier2(y1, n1):
    b = y1.shape[0]
    rows = 16
    m2 = jnp.asarray(_stage2_matrices(n1)).astype(BF16)
    return pl.pallas_call(
        functools.partial(_fourier2_kernel, rows=rows),
        grid=(b, n1 // rows),
        in_specs=[
            pl.BlockSpec((None, 2, SEQ_MINOR, rows, FOURIER_WIDTH), lambda i, j: (i, 0, 0, j, 0)),
            pl.BlockSpec((rows, SEQ_MINOR, 2 * SEQ_MINOR), lambda i, j: (j, 0, 0)),
        ],
        out_specs=pl.BlockSpec((None, SEQ_MINOR, rows, FOURIER_WIDTH), lambda i, j: (i, 0, j, 0)),
        out_shape=jax.ShapeDtypeStruct((b, SEQ_MINOR, n1, FOURIER_WIDTH), BF16),
        compiler_params=_cparams(("arbitrary", "arbitrary")),
        name="fourier_stage2",
    )(y1, m2)


def _t5_bucket(rel):
    nb = NUM_BUCKETS // 2
    max_exact = nb // 2
    ret = (rel > 0).astype(jnp.int32) * nb
    n = jnp.abs(rel)
    nf = jnp.maximum(n, 1).astype(jnp.float32)
    large = max_exact + (jnp.log(nf / max_exact) / math.log(MAX_DISTANCE / max_exact)
                         * (nb - max_exact)).astype(jnp.int32)
    large = jnp.minimum(large, nb - 1)
    return ret + jnp.where(n < max_exact, n, large)


def _bias_tables(rel_bias):
    tq, tk = ATTN_Q_TILE, ATTN_KV_TILE
    r = tq // tk
    wd = tq + tk
    mp = np.arange(wd)
    m = np.where(mp < tq, mp, mp - wd)
    d = (np.arange(r + 2)[:, None] - 1) * tk - m[None, :]
    tbl = rel_bias.astype(F32) * LOG2E
    w = tbl[_t5_bucket(jnp.asarray(d, jnp.int32))]
    wtab = jnp.transpose(w, (2, 0, 1))[:, :, None, :]
    far = tbl[_t5_bucket(jnp.asarray([-2 * tk, 2 * tk], jnp.int32))]
    cfar = jnp.broadcast_to(far.T[:, :, None, None], (N_HEADS, 2, 1, tq))
    return wtab, cfar


def _attn_kernel(lam_ref, qt_ref, k_ref, vt_ref, w_ref, cfar_ref, g_ref, o_ref, m_sc, l_sc, acc_sc, bias_sc, ex_sc,
                 *, nkv, tq, tk):
    qi = pl.program_id(2)
    r = tq // tk
    w = ATTN_LANE_CHUNK
    chunks = tq // w
    pieces = tk // TOKEN_TILE

    @pl.when((pl.program_id(1) == 0) & (qi == 0))
    def _():
        for t in range(r + 2):
            x = jnp.broadcast_to(w_ref[t], (tk, tq + tk))
            bias_sc[t] = pltpu.roll(x, 0, 1, stride=1, stride_axis=0)[:, :tq]

    row = lax.broadcasted_iota(jnp.int32, (ATTN_HEAD, w), 0)
    qts = []
    for lc in range(chunks):
        q = qt_ref[(lc * w) // TOKEN_TILE][:, (lc * w) % TOKEN_TILE:(lc * w) % TOKEN_TILE + w]
        zero = jnp.zeros_like(q)
        qts.append((jnp.where(row < HALF_DIM, q, zero), jnp.where(row >= HALF_DIM, q, zero)))

    lo = jnp.maximum(qi * r - 1, 0)
    hi = jnp.minimum((qi + 1) * r + 1, nkv)

    def logits(j, bias_view, c, lc):
        kj = k_ref[pl.ds(pl.multiple_of(j * tk, tk), tk), :]
        s = jnp.dot(kj, qts[lc][c], preferred_element_type=F32)
        if bias_view is not None:
            s = jnp.maximum(s, NEG_BIG) + bias_view[:, lc * w:(lc + 1) * w]
        return s

    def pv(j, p):
        pb = p.astype(BF16)
        out = jnp.dot(vt_ref[j * pieces], pb[:TOKEN_TILE], preferred_element_type=F32)
        for i in range(1, pieces):
            out += jnp.dot(vt_ref[j * pieces + i], pb[i * TOKEN_TILE:(i + 1) * TOKEN_TILE],
                           preferred_element_type=F32)
        return out

    def safe_block(j, bias_view, c_row):
        ss = [[logits(j, bias_view, c, lc) for lc in range(chunks)] for c in range(2)]
        for c in range(2):
            for lc in range(chunks):
                sl = slice(lc * w, (lc + 1) * w)
                s = ss[c][lc]
                m_old = m_sc[c, :, sl]
                if bias_view is not None:
                    m_new = jnp.maximum(m_old, jnp.max(s, axis=0, keepdims=True))
                    p = jnp.exp2(s - m_new)
                else:
                    cr = c_row[:, sl]
                    m_new = jnp.maximum(m_old, jnp.max(s, axis=0, keepdims=True) + cr)
                    p = jnp.exp2(s - (m_new - cr))
                alpha = jnp.exp2(m_old - m_new)
                l_sc[c, :, sl] = alpha * l_sc[c, :, sl] + jnp.sum(p, axis=0, keepdims=True)
                acc_sc[c, :, sl] = alpha * acc_sc[c, :, sl] + pv(j, p)
                m_sc[c, :, sl] = m_new

    def fast_block(j, tile, c_row):
        order = [(c, lc) for c in range(2) for lc in range(chunks)]
        ahead = len(order) if tile is None else ATTN_QK_AHEAD_NEAR
        pending = {key: logits(j, None, *key) for key in order[:ahead]}
        for idx, (c, lc) in enumerate(order):
            if idx + ahead < len(order):
                nxt = order[idx + ahead]
                pending[nxt] = logits(j, None, *nxt)
            sl = slice(lc * w, (lc + 1) * w)
            s = pending.pop((c, lc))
            m_old = m_sc[c, :, sl]
            if tile is not None:
                es = []
                for a in range(tk // w):
                    rows = slice(a * w, (a + 1) * w)
                    dist = (tile - 1) * tk + (a - lc) * w
                    if abs(dist) >= w - 1 + MAX_DISTANCE:
                        side = cfar_ref[0 if dist < 0 else 1][:, sl]
                        es.append(s[rows] - (m_old - side))
                    else:
                        es.append((s[rows] - m_old) + bias_sc[tile, rows, sl])
                e = jnp.concatenate(es, axis=0)
                p = jnp.exp2(e)
                bm = jnp.max(e, axis=0, keepdims=True) + m_old
            else:
                cr = c_row[:, sl]
                p = jnp.exp2(s - (m_old - cr))
                bm = jnp.max(s, axis=0, keepdims=True) + cr
            m_new = jnp.maximum(m_old, bm)
            alpha = jnp.exp2(m_old - m_new)
            ex_sc[c, :, sl] = jnp.maximum(ex_sc[c, :, sl], bm - m_old)
            l_sc[c, :, sl] = alpha * (l_sc[c, :, sl] + jnp.sum(p, axis=0, keepdims=True))
            acc_sc[c, :, sl] = alpha * (acc_sc[c, :, sl] + pv(j, p))
            m_sc[c, :, sl] = m_new

    def run(block_fn, pair_far):
        def far(c_row):
            def body(j, carry):
                block_fn(j, None, c_row)
                return carry
            return body

        def near(j, carry):
            block_fn(j, bias_sc.at[j - (qi * r - 1)], None)
            return carry

        def far_pair(c_row):
            def body(i, first):
                block_fn(first + 2 * i, None, c_row)
                block_fn(first + 2 * i + 1, None, c_row)
                return first
            return body

        if pair_far:
            n_lo = lo // 2
            lax.fori_loop(0, n_lo, far_pair(cfar_ref[0]), 0)
            lax.fori_loop(2 * n_lo, lo, far(cfar_ref[0]), 0)
            for t in range(r + 2):
                j_t = qi * r - 1 + t

                @pl.when((j_t >= 0) & (j_t < nkv))
                def _():
                    block_fn(j_t, t, None)
            n_hi = (nkv - hi) // 2
            lax.fori_loop(0, n_hi, far_pair(cfar_ref[1]), hi)
            lax.fori_loop(hi + 2 * n_hi, nkv, far(cfar_ref[1]), 0)
        else:
            lax.fori_loop(0, lo, far(cfar_ref[0]), 0)
            lax.fori_loop(lo, hi, near, 0)
            lax.fori_loop(hi, nkv, far(cfar_ref[1]), 0)

    l_sc[...] = jnp.zeros(l_sc.shape, F32)
    acc_sc[...] = jnp.zeros(acc_sc.shape, F32)
    ex_sc[...] = jnp.zeros(ex_sc.shape, F32)

    k0 = k_ref[0:ATTN_LEVEL_ROWS, :]

    @pl.when(qi == 0)
    def _():
        for c in range(2):
            for lc in range(chunks):
                sl = slice(lc * w, (lc + 1) * w)
                s0 = jnp.dot(k0, qts[lc][c], preferred_element_type=F32)
                s0 = jnp.maximum(s0, NEG_BIG) + bias_sc[1, 0:ATTN_LEVEL_ROWS, sl]
                m_sc[c, :, sl] = jnp.max(s0, axis=0, keepdims=True)

    @pl.when(qi != 0)
    def _():
        for c in range(2):
            for lc in range(chunks):
                sl = slice(lc * w, (lc + 1) * w)
                s0 = jnp.dot(k0, qts[lc][c], preferred_element_type=F32)
                m_sc[c, :, sl] = jnp.max(s0, axis=0, keepdims=True) + cfar_ref[0][:, sl]

    run(fast_block, True)

    @pl.when(jnp.max(ex_sc[...]) > ATTN_EXP_GUARD)
    def _():
        m_sc[...] = jnp.full(m_sc.shape, NEG_BIG, F32)
        l_sc[...] = jnp.zeros(l_sc.shape, F32)
        acc_sc[...] = jnp.zeros(acc_sc.shape, F32)
        run(safe_block, False)

    lam = (jnp.exp(jnp.sum(lam_ref[0:1, :] * lam_ref[1:2, :], axis=-1, keepdims=True))
           - jnp.exp(jnp.sum(lam_ref[2:3, :] * lam_ref[3:4, :], axis=-1, keepdims=True))
           + LAM_INIT)
    o = acc_sc[0] / l_sc[0] - lam * (acc_sc[1] / l_sc[1])
    ms = jnp.mean(o * o, axis=0, keepdims=True)
    o = o * lax.rsqrt(ms + EPS) * g_ref[...]
    o = o * (1.0 - LAM_INIT)
    o_ref[...] = o.T.astype(BF16)


def _attention(qt, k, vt, wtab, cfar, lam_params, subln_g, b, s):
    tq, tk = ATTN_Q_TILE, ATTN_KV_TILE
    r = tq // tk
    nq = s // tq
    nkv = s // tk
    t = b * s
    return pl.pallas_call(
        functools.partial(_attn_kernel, nkv=nkv, tq=tq, tk=tk),
        grid=(N_HEADS, b, nq),
        in_specs=[
            pl.BlockSpec((4, HALF_DIM), lambda h, bi, qi: (0, 0)),
            pl.BlockSpec((tq // TOKEN_TILE, ATTN_HEAD, TOKEN_TILE), lambda h, bi, qi: (bi * nq + qi, h, 0)),
            pl.BlockSpec((s, ATTN_HEAD), lambda h, bi, qi: (bi, h)),
            pl.BlockSpec((s // TOKEN_TILE, ATTN_HEAD, TOKEN_TILE), lambda h, bi, qi: (bi, h, 0)),
            pl.BlockSpec((None, r + 2, 1, tq + tk), lambda h, bi, qi: (h, 0, 0, 0)),
            pl.BlockSpec((None, 2, 1, tq), lambda h, bi, qi: (h, 0, 0, 0)),
            pl.BlockSpec((ATTN_HEAD, 1), lambda h, bi, qi: (0, 0)),
        ],
        out_specs=pl.BlockSpec((tq, ATTN_HEAD), lambda h, bi, qi: (bi * nq + qi, h)),
        out_shape=jax.ShapeDtypeStruct((t, ATTN_WIDTH), BF16),
        scratch_shapes=[
            pltpu.VMEM((2, 1, tq), F32),
            pltpu.VMEM((2, 1, tq), F32),
            pltpu.VMEM((2, ATTN_HEAD, tq), F32),
            pltpu.VMEM((r + 2, tk, tq), F32),
            pltpu.VMEM((2, 1, tq), F32),
        ],
        compiler_params=_cparams(("arbitrary", "arbitrary", "arbitrary")),
        name="diff_attention",
    )(lam_params, qt, k, vt, wtab, cfar, subln_g.reshape(ATTN_HEAD, 1))


def _outproj_kernel(x_ref, yf_ref, ya_ref, w_ref, g_ref, x1_ref, h2_ref):
    y = (jnp.dot(yf_ref[...], w_ref[:FOURIER_WIDTH, :], preferred_element_type=F32)
         + jnp.dot(ya_ref[...], w_ref[FOURIER_WIDTH:, :], preferred_element_type=F32))
    x1 = x_ref[...] + y
    x1_ref[...] = x1
    ms = jnp.mean(x1 * x1, axis=-1, keepdims=True)
    h2_ref[...] = (x1 * lax.rsqrt(ms + EPS) * g_ref[...]).astype(BF16)


def _outproj(x2d, yf, ya, w_out_bf, g2):
    t = x2d.shape[0]
    tm = TOKEN_TILE
    return pl.pallas_call(
        _outproj_kernel,
        grid=(t // tm,),
        in_specs=[
            pl.BlockSpec((tm, D_MODEL), lambda i: (i, 0)),
            pl.BlockSpec((tm, FOURIER_WIDTH), lambda i: (i, 0)),
            pl.BlockSpec((tm, ATTN_WIDTH), lambda i: (i, 0)),
            pl.BlockSpec((D_MODEL, D_MODEL), lambda i: (0, 0)),
            pl.BlockSpec((1, D_MODEL), lambda i: (0, 0)),
        ],
        out_specs=[
            pl.BlockSpec((tm, D_MODEL), lambda i: (i, 0)),
            pl.BlockSpec((tm, D_MODEL), lambda i: (i, 0)),
        ],
        out_shape=[
            jax.ShapeDtypeStruct((t, D_MODEL), F32),
            jax.ShapeDtypeStruct((t, D_MODEL), BF16),
        ],
        compiler_params=_cparams(("arbitrary",)),
        name="outproj",
    )(x2d, yf, ya, w_out_bf, g2.reshape(1, D_MODEL))


def _ffn_kernel(h_ref, x1_ref, wg_ref, wu_ref, wd_ref, g_ref, o_ref, acc_ref):
    f = pl.program_id(1)

    @pl.when(f == 0)
    def _():
        acc_ref[...] = jnp.zeros(acc_ref.shape, F32)

    h = h_ref[...]
    gate = jnp.dot(h, wg_ref[...], preferred_element_type=F32)
    up = jnp.dot(h, wu_ref[...], preferred_element_type=F32)
    a = (gate * (1.0 / (1.0 + jnp.exp(-gate))) * up).astype(BF16)
    acc_ref[...] += jnp.dot(a, wd_ref[...], preferred_element_type=F32)

    @pl.when(f == pl.num_programs(1) - 1)
    def _():
        x2 = x1_ref[...] + acc_ref[...]
        ms = jnp.mean(x2 * x2, axis=-1, keepdims=True)
        o_ref[...] = x2 * lax.rsqrt(ms + EPS) * g_ref[...]


def _ffn(h2, x1, wg_bf, wu_bf, wd_bf, gf):
    t = h2.shape[0]
    tm = TOKEN_TILE
    tf = FF_TILE
    return pl.pallas_call(
        _ffn_kernel,
        grid=(t // tm, D_FF // tf),
        in_specs=[
            pl.BlockSpec((tm, D_MODEL), lambda i, f: (i, 0)),
            pl.BlockSpec((tm, D_MODEL), lambda i, f: (i, 0)),
            pl.BlockSpec((D_MODEL, tf), lambda i, f: (0, f)),
            pl.BlockSpec((D_MODEL, tf), lambda i, f: (0, f)),
            pl.BlockSpec((tf, D_MODEL), lambda i, f: (f, 0)),
            pl.BlockSpec((1, D_MODEL), lambda i, f: (0, 0)),
        ],
        out_specs=pl.BlockSpec((tm, D_MODEL), lambda i, f: (i, 0)),
        out_shape=jax.ShapeDtypeStruct((t, D_MODEL), F32),
        scratch_shapes=[pltpu.VMEM((tm, D_MODEL), F32)],
        compiler_params=_cparams(("arbitrary", "arbitrary")),
        name="ffn",
    )(h2, x1, wg_bf, wu_bf, wd_bf, gf.reshape(1, D_MODEL))


def _encoder(x, params):
    b, s, _ = x.shape
    t = b * s
    n1 = s // SEQ_MINOR
    x2d = x.reshape(t, D_MODEL)

    pq, qt, k, vt = _inproj(x2d, params["norm1_g"], params["w_in"], params["ab"])
    y1 = _fourier1(pq.reshape(2, b, n1, SEQ_MINOR, FOURIER_WIDTH), n1)
    yf = _fourier2(y1, n1).reshape(t, FOURIER_WIDTH)
    ya = _attention(qt, k, vt, params["wtab"], params["cfar"], params["lam"], params["subln_g"], b, s)
    x1, h2 = _outproj(x2d, yf, ya, params["w_out"], params["norm2_g"])
    out = _ffn(h2, x1, params["w_gate"], params["w_up"], params["w_down"], params["final_g"])
    return out.reshape(b, s, D_MODEL)


def kernel(x_prompt, x_sample, norm1_g, w_in, w_fourier, lambda_q1, lambda_k1, lambda_q2, lambda_k2,
           subln_g, w_out, norm2_g, w_gate, w_up, w_down, rel_bias, final_g):
    wtab, cfar = _bias_tables(rel_bias)
    params = {
        "norm1_g": norm1_g[0].astype(F32),
        "w_in": w_in[0].astype(BF16),
        "ab": _fourier_weights(w_fourier[0].astype(F32)),
        "lam": jnp.stack([lambda_q1[0], lambda_k1[0], lambda_q2[0], lambda_k2[0]]).astype(F32),
        "subln_g": subln_g[0].astype(F32),
        "w_out": w_out[0].astype(BF16),
        "norm2_g": norm2_g[0].astype(F32),
        "w_gate": w_gate[0].astype(BF16),
        "w_up": w_up[0].astype(BF16),
        "w_down": w_down[0].astype(BF16),
        "wtab": wtab,
        "cfar": cfar,
        "final_g": final_g.astype(F32),
    }
    return (_encoder(x_prompt, params), _encoder(x_sample, params))
```

```python
import functools
import math

import numpy as np
import jax
import jax.numpy as jnp
from jax import lax
from jax.experimental import pallas as pl
from jax.experimental.pallas import tpu as pltpu

F32 = jnp.float32
BF16 = jnp.bfloat16

D_MODEL = 2048
FOURIER_WIDTH = 1024
FOURIER_GROUP = 128
N_FGROUPS = 8
ATTN_WIDTH = 1024
ATTN_HEAD = 128
N_HEADS = 8
HALF_DIM = 64
D_FF = 5632
NUM_BUCKETS = 32
MAX_DISTANCE = 128
EPS = 1e-6
LAM_INIT = 0.8 - 0.6 * math.exp(-0.3 * 0)

SEQ_MINOR = 128
TOKEN_TILE = 512
FF_TILE = 512
ATTN_Q_TILE = 2048
ATTN_KV_TILE = 1024
ATTN_EXP_GUARD = 64.0
ATTN_LEVEL_ROWS = 16
ATTN_QK_AHEAD_NEAR = 4
ATTN_LANE_CHUNK = 256
LOG2E = math.log2(math.e)
NEG_BIG = -1e30
VMEM_LIMIT = 56 * 1024 * 1024


def _cparams(sem):
    return pltpu.CompilerParams(dimension_semantics=sem, vmem_limit_bytes=VMEM_LIMIT)


def _dft128_tables():
    n = np.arange(FOURIER_GROUP)
    ang = 2.0 * np.pi * ((n[:, None] * n[None, :]) % FOURIER_GROUP) / FOURIER_GROUP
    return np.stack([np.cos(ang), np.sin(ang)]).astype(np.float32)


def _stage1_matrix(n1):
    n = np.arange(n1)
    ang = 2.0 * np.pi * ((n[:, None] * n[None, :]) % n1) / n1
    c, s = np.cos(ang), np.sin(ang)
    return np.block([[c, -s], [-s, -c]]).astype(np.float32)


def _stage2_matrices(n1):
    s_len = n1 * SEQ_MINOR
    s1p = np.arange(n1)[:, None, None]
    s2p = np.arange(SEQ_MINOR)[None, :, None]
    s2 = np.arange(SEQ_MINOR)[None, None, :]
    ang = 2.0 * np.pi * ((s2 * (s1p + n1 * s2p)) % s_len) / s_len
    scale = 1.0 / math.sqrt(s_len * FOURIER_GROUP)
    return (np.concatenate([np.cos(ang), np.sin(ang)], axis=-1) * scale).astype(np.float32)


def _fourier_weight_kernel(dft_ref, w_ref, ab_ref):
    for g in range(N_FGROUPS):
        w = w_ref[g]
        a = jnp.dot(dft_ref[0], w, preferred_element_type=F32, precision=lax.Precision.HIGHEST)
        b = jnp.dot(dft_ref[1], w, preferred_element_type=F32, precision=lax.Precision.HIGHEST)
        ab_ref[g, :, :FOURIER_GROUP] = a.astype(BF16)
        ab_ref[g, :, FOURIER_GROUP:] = b.astype(BF16)


def _fourier_weights(w_f):
    return pl.pallas_call(
        _fourier_weight_kernel,
        out_shape=jax.ShapeDtypeStruct((N_FGROUPS, FOURIER_GROUP, 2 * FOURIER_GROUP), BF16),
        name="fourier_weights",
    )(jnp.asarray(_dft128_tables()), w_f)


def _inproj_kernel(x_ref, g_ref, w_ref, ab_ref, pq_ref, qt_ref, k_ref, vt_ref):
    x = x_ref[...]
    ms = jnp.mean(x * x, axis=-1, keepdims=True)
    h = (x * lax.rsqrt(ms + EPS) * g_ref[...]).astype(BF16)

    def proj(col):
        return jnp.dot(h, w_ref[:, col:col + ATTN_WIDTH], preferred_element_type=F32)

    ub = proj(0).astype(BF16)
    for g in range(N_FGROUPS):
        lo, hi = g * FOURIER_GROUP, (g + 1) * FOURIER_GROUP
        pq = jnp.dot(ub[:, lo:hi], ab_ref[g], preferred_element_type=F32)
        pq_ref[0, :, lo:hi] = pq[:, :FOURIER_GROUP].astype(BF16)
        pq_ref[1, :, lo:hi] = pq[:, FOURIER_GROUP:].astype(BF16)
    qt_ref[0] = (proj(FOURIER_WIDTH) * (HALF_DIM ** -0.5 * LOG2E)).T.astype(BF16)
    k_ref[...] = proj(FOURIER_WIDTH + ATTN_WIDTH).astype(BF16)
    vt_ref[0] = proj(FOURIER_WIDTH + 2 * ATTN_WIDTH).T.astype(BF16)


def _inproj(x2d, g, w_in_bf, ab):
    t = x2d.shape[0]
    tm = TOKEN_TILE
    nt = t // tm
    return pl.pallas_call(
        _inproj_kernel,
        grid=(nt,),
        in_specs=[
            pl.BlockSpec((tm, D_MODEL), lambda i: (i, 0)),
            pl.BlockSpec((1, D_MODEL), lambda i: (0, 0)),
            pl.BlockSpec((D_MODEL, FOURIER_WIDTH + 3 * ATTN_WIDTH), lambda i: (0, 0), pipeline_mode=pl.Buffered(1)),
            pl.BlockSpec((N_FGROUPS, FOURIER_GROUP, 2 * FOURIER_GROUP), lambda i: (0, 0, 0)),
        ],
        out_specs=[
            pl.BlockSpec((2, tm, FOURIER_WIDTH), lambda i: (0, i, 0)),
            pl.BlockSpec((1, ATTN_WIDTH, tm), lambda i: (i, 0, 0)),
            pl.BlockSpec((tm, ATTN_WIDTH), lambda i: (i, 0)),
            pl.BlockSpec((1, ATTN_WIDTH, tm), lambda i: (i, 0, 0)),
        ],
        out_shape=[
            jax.ShapeDtypeStruct((2, t, FOURIER_WIDTH), BF16),
            jax.ShapeDtypeStruct((nt, ATTN_WIDTH, tm), BF16),
            jax.ShapeDtypeStruct((t, ATTN_WIDTH), BF16),
            jax.ShapeDtypeStruct((nt, ATTN_WIDTH, tm), BF16),
        ],
        compiler_params=_cparams(("arbitrary",)),
        name="inproj",
    )(x2d, g.reshape(1, D_MODEL), w_in_bf, ab)


def _fourier1_kernel(x_ref, d_ref, o_ref, *, n1, rows):
    x = pltpu.einshape("abc->bac", x_ref[...].reshape(2 * n1, rows, FOURIER_WIDTH))
    d = d_ref[...]
    for r in range(rows):
        y = jnp.dot(d, x[r], preferred_element_type=F32)
        o_ref[0, r] = y[:n1].astype(BF16)
        o_ref[1, r] = y[n1:].astype(BF16)


def _fourier1(pq5, n1):
    b = pq5.shape[1]
    rows = 16 if n1 >= 64 else 64
    d1 = jnp.asarray(_stage1_matrix(n1)).astype(BF16)
    return pl.pallas_call(
        functools.partial(_fourier1_kernel, n1=n1, rows=rows),
        grid=(b, SEQ_MINOR // rows),
        in_specs=[
            pl.BlockSpec((2, None, n1, rows, FOURIER_WIDTH), lambda i, j: (0, i, 0, j, 0)),
            pl.BlockSpec((2 * n1, 2 * n1), lambda i, j: (0, 0)),
        ],
        out_specs=pl.BlockSpec((None, 2, rows, n1, FOURIER_WIDTH), lambda i, j: (i, 0, j, 0, 0)),
        out_shape=jax.ShapeDtypeStruct((b, 2, SEQ_MINOR, n1, FOURIER_WIDTH), BF16),
        compiler_params=_cparams(("arbitrary", "arbitrary")),
        name="fourier_stage1",
    )(pq5, d1)


def _fourier2_kernel(x_ref, m_ref, o_ref, *, rows):
    x = pltpu.einshape("abc->bac", x_ref[...].reshape(2 * SEQ_MINOR, rows, FOURIER_WIDTH))
    ys = [jnp.dot(m_ref[i], x[i], preferred_element_type=F32).astype(BF16) for i in range(rows)]
    o_ref[...] = pltpu.einshape("abc->bac", jnp.stack(ys))


def _fourier2(y1, n1):
    b = y1.shape[0]
    rows = 16
    m2 = jnp.asarray(_stage2_matrices(n1)).astype(BF16)
    return pl.pallas_call(
        functools.partial(_fourier2_kernel, rows=rows),
        grid=(b, n1 // rows),
        in_specs=[
            pl.BlockSpec((None, 2, SEQ_MINOR, rows, FOURIER_WIDTH), lambda i, j: (i, 0, 0, j, 0)),
            pl.BlockSpec((rows, SEQ_MINOR, 2 * SEQ_MINOR), lambda i, j: (j, 0, 0)),
        ],
        out_specs=pl.BlockSpec((None, SEQ_MINOR, rows, FOURIER_WIDTH), lambda i, j: (i, 0, j, 0)),
        out_shape=jax.ShapeDtypeStruct((b, SEQ_MINOR, n1, FOURIER_WIDTH), BF16),
        compiler_params=_cparams(("arbitrary", "arbitrary")),
        name="fourier_stage2",
    )(y1, m2)


def _t5_bucket(rel):
    nb = NUM_BUCKETS // 2
    max_exact = nb // 2
    ret = (rel > 0).astype(jnp.int32) * nb
    n = jnp.abs(rel)
    nf = jnp.maximum(n, 1).astype(jnp.float32)
    large = max_exact + (jnp.log(nf / max_exact) / math.log(MAX_DISTANCE / max_exact)
                         * (nb - max_exact)).astype(jnp.int32)
    large = jnp.minimum(large, nb - 1)
    return ret + jnp.where(n < max_exact, n, large)


def _bias_tables(rel_bias):
    tq, w = ATTN_Q_TILE, ATTN_LANE_CHUNK
    mp = np.arange(2 * w)
    m = np.where(mp < w, mp, mp - 2 * w)
    d = (np.arange(3)[:, None] - 1) * w - m[None, :]
    tbl = rel_bias.astype(F32) * LOG2E
    g = tbl[_t5_bucket(jnp.asarray(d, jnp.int32))]
    wtab = jnp.transpose(g, (2, 0, 1))[:, :, None, :]
    far = tbl[_t5_bucket(jnp.asarray([-2 * MAX_DISTANCE, 2 * MAX_DISTANCE], jnp.int32))]
    cfar = jnp.broadcast_to(far.T[:, :, None, None], (N_HEADS, 2, 1, tq))
    return wtab, cfar


def _attn_kernel(lam_ref, qt_ref, k_ref, vt_ref, w_ref, cfar_ref, g_ref, o_ref, m_sc, l_sc, acc_sc, bias_sc, ex_sc,
                 *, nkv, tq, tk):
    qi = pl.program_id(2)
    r = tq // tk
    w = ATTN_LANE_CHUNK
    chunks = tq // w
    pieces = tk // TOKEN_TILE

    @pl.when((pl.program_id(1) == 0) & (qi == 0))
    def _():
        for u in range(3):
            x = jnp.broadcast_to(w_ref[u], (w, 2 * w))
            bias_sc[u] = pltpu.roll(x, 0, 1, stride=1, stride_axis=0)[:, :w]

    def sub_bias(tile, a, lc):
        dist = (tile - 1) * tk + (a - lc) * w
        if abs(dist) >= w - 1 + MAX_DISTANCE:
            return None, cfar_ref[0 if dist < 0 else 1][:, lc * w:(lc + 1) * w]
        return dist // w + 1, None

    row = lax.broadcasted_iota(jnp.int32, (ATTN_HEAD, w), 0)
    qts = []
    for lc in range(chunks):
        q = qt_ref[(lc * w) // TOKEN_TILE][:, (lc * w) % TOKEN_TILE:(lc * w) % TOKEN_TILE + w]
        zero = jnp.zeros_like(q)
        qts.append((jnp.where(row < HALF_DIM, q, zero), jnp.where(row >= HALF_DIM, q, zero)))

    lo = jnp.maximum(qi * r - 1, 0)
    hi = jnp.minimum((qi + 1) * r + 1, nkv)

    def logits(j, c, lc):
        kj = k_ref[pl.ds(pl.multiple_of(j * tk, tk), tk), :]
        return jnp.dot(kj, qts[lc][c], preferred_element_type=F32)

    def pv(j, p):
        pb = p.astype(BF16)
        out = jnp.dot(vt_ref[j * pieces], pb[:TOKEN_TILE], preferred_element_type=F32)
        for i in range(1, pieces):
            out += jnp.dot(vt_ref[j * pieces + i], pb[i * TOKEN_TILE:(i + 1) * TOKEN_TILE],
                           preferred_element_type=F32)
        return out

    def safe_block(j, tile, c_row):
        ss = [[logits(j, c, lc) for lc in range(chunks)] for c in range(2)]
        for c in range(2):
            for lc in range(chunks):
                sl = slice(lc * w, (lc + 1) * w)
                s = ss[c][lc]
                m_old = m_sc[c, :, sl]
                if tile is not None:
                    ts = []
                    for a in range(tk // w):
                        u, side = sub_bias(tile, a, lc)
                        sa = jnp.maximum(s[a * w:(a + 1) * w], NEG_BIG)
                        ts.append(sa + (side if u is None else bias_sc[u]))
                    s = jnp.concatenate(ts, axis=0)
                    m_new = jnp.maximum(m_old, jnp.max(s, axis=0, keepdims=True))
                    p = jnp.exp2(s - m_new)
                else:
                    cr = c_row[:, sl]
                    m_new = jnp.maximum(m_old, jnp.max(s, axis=0, keepdims=True) + cr)
                    p = jnp.exp2(s - (m_new - cr))
                alpha = jnp.exp2(m_old - m_new)
                l_sc[c, :, sl] = alpha * l_sc[c, :, sl] + jnp.sum(p, axis=0, keepdims=True)
                acc_sc[c, :, sl] = alpha * acc_sc[c, :, sl] + pv(j, p)
                m_sc[c, :, sl] = m_new

    def fast_block(j, tile, c_row):
        order = [(c, lc) for c in range(2) for lc in range(chunks)]
        ahead = len(order) if tile is None else ATTN_QK_AHEAD_NEAR
        pending = {key: logits(j, *key) for key in order[:ahead]}
        for idx, (c, lc) in enumerate(order):
            if idx + ahead < len(order):
                nxt = order[idx + ahead]
                pending[nxt] = logits(j, *nxt)
            sl = slice(lc * w, (lc + 1) * w)
            s = pending.pop((c, lc))
            m_old = m_sc[c, :, sl]
            if tile is not None:
                es = []
                for a in range(tk // w):
                    rows = slice(a * w, (a + 1) * w)
                    u, side = sub_bias(tile, a, lc)
                    if u is None:
                        es.append(s[rows] - (m_old - side))
                    else:
                        es.append((s[rows] - m_old) + bias_sc[u])
                e = jnp.concatenate(es, axis=0)
                p = jnp.exp2(e)
                bm = jnp.max(e, axis=0, keepdims=True) + m_old
            else:
                cr = c_row[:, sl]
                p = jnp.exp2(s - (m_old - cr))
                bm = jnp.max(s, axis=0, keepdims=True) + cr
            m_new = jnp.maximum(m_old, bm)
            alpha = jnp.exp2(m_old - m_new)
            ex_sc[c, :, sl] = jnp.maximum(ex_sc[c, :, sl], bm - m_old)
            l_sc[c, :, sl] = alpha * (l_sc[c, :, sl] + jnp.sum(p, axis=0, keepdims=True))
            acc_sc[c, :, sl] = alpha * (acc_sc[c, :, sl] + pv(j, p))
            m_sc[c, :, sl] = m_new

    def run(block_fn, pair_far):
        def far(c_row):
            def body(j, carry):
                block_fn(j, None, c_row)
                return carry
            return body

        def near():
            for t in range(r + 2):
                j_t = qi * r - 1 + t

                @pl.when((j_t >= 0) & (j_t < nkv))
                def _():
                    block_fn(j_t, t, None)

        def far_pair(c_row):
            def body(i, first):
                block_fn(first + 2 * i, None, c_row)
                block_fn(first + 2 * i + 1, None, c_row)
                return first
            return body

        if pair_far:
            n_lo = lo // 2
            lax.fori_loop(0, n_lo, far_pair(cfar_ref[0]), 0)
            lax.fori_loop(2 * n_lo, lo, far(cfar_ref[0]), 0)
            near()
            n_hi = (nkv - hi) // 2
            lax.fori_loop(0, n_hi, far_pair(cfar_ref[1]), hi)
            lax.fori_loop(hi + 2 * n_hi, nkv, far(cfar_ref[1]), 0)
        else:
            lax.fori_loop(0, lo, far(cfar_ref[0]), 0)
            near()
            lax.fori_loop(hi, nkv, far(cfar_ref[1]), 0)

    l_sc[...] = jnp.zeros(l_sc.shape, F32)
    acc_sc[...] = jnp.zeros(acc_sc.shape, F32)
    ex_sc[...] = jnp.zeros(ex_sc.shape, F32)

    k0 = k_ref[0:ATTN_LEVEL_ROWS, :]

    @pl.when(qi == 0)
    def _():
        for c in range(2):
            for lc in range(chunks):
                sl = slice(lc * w, (lc + 1) * w)
                s0 = jnp.maximum(jnp.dot(k0, qts[lc][c], preferred_element_type=F32), NEG_BIG)
                u, side = sub_bias(1, 0, lc)
                s0 = s0 + (side if u is None else bias_sc[u, 0:ATTN_LEVEL_ROWS, :])
                m_sc[c, :, sl] = jnp.max(s0, axis=0, keepdims=True)

    @pl.when(qi != 0)
    def _():
        for c in range(2):
            for lc in range(chunks):
                sl = slice(lc * w, (lc + 1) * w)
                s0 = jnp.dot(k0, qts[lc][c], preferred_element_type=F32)
                m_sc[c, :, sl] = jnp.max(s0, axis=0, keepdims=True) + cfar_ref[0][:, sl]

    run(fast_block, True)

    @pl.when(jnp.max(ex_sc[...]) > ATTN_EXP_GUARD)
    def _():
        m_sc[...] = jnp.full(m_sc.shape, NEG_BIG, F32)
        l_sc[...] = jnp.zeros(l_sc.shape, F32)
        acc_sc[...] = jnp.zeros(acc_sc.shape, F32)
        run(safe_block, False)

    lam = (jnp.exp(jnp.sum(lam_ref[0:1, :] * lam_ref[1:2, :], axis=-1, keepdims=True))
           - jnp.exp(jnp.sum(lam_ref[2:3, :] * lam_ref[3:4, :], axis=-1, keepdims=True))
           + LAM_INIT)
    o = acc_sc[0] / l_sc[0] - lam * (acc_sc[1] / l_sc[1])
    ms = jnp.mean(o * o, axis=0, keepdims=True)
    o = o * lax.rsqrt(ms + EPS) * g_ref[...]
    o = o * (1.0 - LAM_INIT)
    o_ref[...] = o.T.astype(BF16)


def _attention(qt, k, vt, wtab, cfar, lam_params, subln_g, b, s):
    tq, tk, w = ATTN_Q_TILE, ATTN_KV_TILE, ATTN_LANE_CHUNK
    nq = s // tq
    nkv = s // tk
    t = b * s
    return pl.pallas_call(
        functools.partial(_attn_kernel, nkv=nkv, tq=tq, tk=tk),
        grid=(N_HEADS, b, nq),
        in_specs=[
            pl.BlockSpec((4, HALF_DIM), lambda h, bi, qi: (0, 0)),
            pl.BlockSpec((tq // TOKEN_TILE, ATTN_HEAD, TOKEN_TILE), lambda h, bi, qi: (bi * nq + qi, h, 0)),
            pl.BlockSpec((s, ATTN_HEAD), lambda h, bi, qi: (bi, h)),
            pl.BlockSpec((s // TOKEN_TILE, ATTN_HEAD, TOKEN_TILE), lambda h, bi, qi: (bi, h, 0)),
            pl.BlockSpec((None, 3, 1, 2 * w), lambda h, bi, qi: (h, 0, 0, 0)),
            pl.BlockSpec((None, 2, 1, tq), lambda h, bi, qi: (h, 0, 0, 0)),
            pl.BlockSpec((ATTN_HEAD, 1), lambda h, bi, qi: (0, 0)),
        ],
        out_specs=pl.BlockSpec((tq, ATTN_HEAD), lambda h, bi, qi: (bi * nq + qi, h)),
        out_shape=jax.ShapeDtypeStruct((t, ATTN_WIDTH), BF16),
        scratch_shapes=[
            pltpu.VMEM((2, 1, tq), F32),
            pltpu.VMEM((2, 1, tq), F32),
            pltpu.VMEM((2, ATTN_HEAD, tq), F32),
            pltpu.VMEM((3, w, w), F32),
            pltpu.VMEM((2, 1, tq), F32),
        ],
        compiler_params=_cparams(("arbitrary", "arbitrary", "arbitrary")),
        name="diff_attention",
    )(lam_params, qt, k, vt, wtab, cfar, subln_g.reshape(ATTN_HEAD, 1))


def _outproj_kernel(x_ref, yf_ref, ya_ref, w_ref, g_ref, x1_ref, h2_ref):
    y = (jnp.dot(yf_ref[...], w_ref[:FOURIER_WIDTH, :], preferred_element_type=F32)
         + jnp.dot(ya_ref[...], w_ref[FOURIER_WIDTH:, :], preferred_element_type=F32))
    x1 = x_ref[...] + y
    x1_ref[...] = x1
    ms = jnp.mean(x1 * x1, axis=-1, keepdims=True)
    h2_ref[...] = (x1 * lax.rsqrt(ms + EPS) * g_ref[...]).astype(BF16)


def _outproj(x2d, yf, ya, w_out_bf, g2):
    t = x2d.shape[0]
    tm = TOKEN_TILE
    return pl.pallas_call(
        _outproj_kernel,
        grid=(t // tm,),
        in_specs=[
            pl.BlockSpec((tm, D_MODEL), lambda i: (i, 0)),
            pl.BlockSpec((tm, FOURIER_WIDTH), lambda i: (i, 0)),
            pl.BlockSpec((tm, ATTN_WIDTH), lambda i: (i, 0)),
            pl.BlockSpec((D_MODEL, D_MODEL), lambda i: (0, 0)),
            pl.BlockSpec((1, D_MODEL), lambda i: (0, 0)),
        ],
        out_specs=[
            pl.BlockSpec((tm, D_MODEL), lambda i: (i, 0)),
            pl.BlockSpec((tm, D_MODEL), lambda i: (i, 0)),
        ],
        out_shape=[
            jax.ShapeDtypeStruct((t, D_MODEL), F32),
            jax.ShapeDtypeStruct((t, D_MODEL), BF16),
        ],
        compiler_params=_cparams(("arbitrary",)),
        name="outproj",
    )(x2d, yf, ya, w_out_bf, g2.reshape(1, D_MODEL))


def _ffn_kernel(h_ref, x1_ref, wg_ref, wu_ref, wd_ref, g_ref, o_ref, acc_ref):
    f = pl.program_id(1)

    @pl.when(f == 0)
    def _():
        acc_ref[...] = jnp.zeros(acc_ref.shape, F32)

    h = h_ref[...]
    gate = jnp.dot(h, wg_ref[...], preferred_element_type=F32)
    up = jnp.dot(h, wu_ref[...], preferred_element_type=F32)
    a = (gate * (1.0 / (1.0 + jnp.exp(-gate))) * up).astype(BF16)
    acc_ref[...] += jnp.dot(a, wd_ref[...], preferred_element_type=F32)

    @pl.when(f == pl.num_programs(1) - 1)
    def _():
        x2 = x1_ref[...] + acc_ref[...]
        ms = jnp.mean(x2 * x2, axis=-1, keepdims=True)
        o_ref[...] = x2 * lax.rsqrt(ms + EPS) * g_ref[...]


def _ffn(h2, x1, wg_bf, wu_bf, wd_bf, gf):
    t = h2.shape[0]
    tm = TOKEN_TILE
    tf = FF_TILE
    return pl.pallas_call(
        _ffn_kernel,
        grid=(t // tm, D_FF // tf),
        in_specs=[
            pl.BlockSpec((tm, D_MODEL), lambda i, f: (i, 0)),
            pl.BlockSpec((tm, D_MODEL), lambda i, f: (i, 0)),
            pl.BlockSpec((D_MODEL, tf), lambda i, f: (0, f)),
            pl.BlockSpec((D_MODEL, tf), lambda i, f: (0, f)),
            pl.BlockSpec((tf, D_MODEL), lambda i, f: (f, 0)),
            pl.BlockSpec((1, D_MODEL), lambda i, f: (0, 0)),
        ],
        out_specs=pl.BlockSpec((tm, D_MODEL), lambda i, f: (i, 0)),
        out_shape=jax.ShapeDtypeStruct((t, D_MODEL), F32),
        scratch_shapes=[pltpu.VMEM((tm, D_MODEL), F32)],
        compiler_params=_cparams(("arbitrary", "arbitrary")),
        name="ffn",
    )(h2, x1, wg_bf, wu_bf, wd_bf, gf.reshape(1, D_MODEL))


def _encoder(x, params):
    b, s, _ = x.shape
    t = b * s
    n1 = s // SEQ_MINOR
    x2d = x.reshape(t, D_MODEL)

    pq, qt, k, vt = _inproj(x2d, params["norm1_g"], params["w_in"], params["ab"])
    y1 = _fourier1(pq.reshape(2, b, n1, SEQ_MINOR, FOURIER_WIDTH), n1)
    yf = _fourier2(y1, n1).reshape(t, FOURIER_WIDTH)
    ya = _attention(qt, k, vt, params["wtab"], params["cfar"], params["lam"], params["subln_g"], b, s)
    x1, h2 = _outproj(x2d, yf, ya, params["w_out"], params["norm2_g"])
    out = _ffn(h2, x1, params["w_gate"], params["w_up"], params["w_down"], params["final_g"])
    return out.reshape(b, s, D_MODEL)


def kernel(x_prompt, x_sample, norm1_g, w_in, w_fourier, lambda_q1, lambda_k1, lambda_q2, lambda_k2,
           subln_g, w_out, norm2_g, w_gate, w_up, w_down, rel_bias, final_g):
    wtab, cfar = _bias_tables(rel_bias)
    params = {
        "norm1_g": norm1_g[0].astype(F32),
        "w_in": w_in[0].astype(BF16),
        "ab": _fourier_weights(w_fourier[0].astype(F32)),
        "lam": jnp.stack([lambda_q1[0], lambda_k1[0], lambda_q2[0], lambda_k2[0]]).astype(F32),
        "subln_g": subln_g[0].astype(F32),
        "w_out": w_out[0].astype(BF16),
        "norm2_g": norm2_g[0].astype(F32),
        "w_gate": w_gate[0].astype(BF16),
        "w_up": w_up[0].astype(BF16),
        "w_down": w_down[0].astype(BF16),
        "wtab": wtab,
        "cfar": cfar,
        "final_g": final_g.astype(F32),
    }
    return (_encoder(x_prompt, params), _encoder(x_sample, params))
```

```python
import functools
import math

import numpy as np
import jax
import jax.numpy as jnp
from jax import lax
from jax.experimental import pallas as pl
from jax.experimental.pallas import tpu as pltpu

F32 = jnp.float32
BF16 = jnp.bfloat16

D_MODEL = 2048
FOURIER_WIDTH = 1024
FOURIER_GROUP = 128
N_FGROUPS = 8
ATTN_WIDTH = 1024
ATTN_HEAD = 128
N_HEADS = 8
HALF_DIM = 64
D_FF = 5632
NUM_BUCKETS = 32
MAX_DISTANCE = 128
EPS = 1e-6
LAM_INIT = 0.8 - 0.6 * math.exp(-0.3 * 0)

SEQ_MINOR = 128
TOKEN_TILE = 512
FF_TILE = 512
ATTN_Q_TILE = 1024
ATTN_Q_TILE_SHORT = 2048
ATTN_KV_TILE = 1024
ATTN_EXP_GUARD = 64.0
ATTN_LEVEL_ROWS = 16
ATTN_QK_AHEAD_NEAR = 4
ATTN_LANE_CHUNK = 256
LOG2E = math.log2(math.e)
NEG_BIG = -1e30
VMEM_LIMIT = 56 * 1024 * 1024


def _cparams(sem):
    return pltpu.CompilerParams(dimension_semantics=sem, vmem_limit_bytes=VMEM_LIMIT)


def _dft128_tables():
    n = np.arange(FOURIER_GROUP)
    ang = 2.0 * np.pi * ((n[:, None] * n[None, :]) % FOURIER_GROUP) / FOURIER_GROUP
    return np.stack([np.cos(ang), np.sin(ang)]).astype(np.float32)


def _stage1_matrix(n1):
    n = np.arange(n1)
    ang = 2.0 * np.pi * ((n[:, None] * n[None, :]) % n1) / n1
    c, s = np.cos(ang), np.sin(ang)
    return np.block([[c, -s], [-s, -c]]).astype(np.float32)


def _stage2_matrices(n1):
    s_len = n1 * SEQ_MINOR
    s1p = np.arange(n1)[:, None, None]
    s2p = np.arange(SEQ_MINOR)[None, :, None]
    s2 = np.arange(SEQ_MINOR)[None, None, :]
    ang = 2.0 * np.pi * ((s2 * (s1p + n1 * s2p)) % s_len) / s_len
    scale = 1.0 / math.sqrt(s_len * FOURIER_GROUP)
    return (np.concatenate([np.cos(ang), np.sin(ang)], axis=-1) * scale).astype(np.float32)


def _fourier_weight_kernel(dft_ref, w_ref, ab_ref):
    for g in range(N_FGROUPS):
        w = w_ref[g]
        a = jnp.dot(dft_ref[0], w, preferred_element_type=F32, precision=lax.Precision.HIGHEST)
        b = jnp.dot(dft_ref[1], w, preferred_element_type=F32, precision=lax.Precision.HIGHEST)
        ab_ref[g, :, :FOURIER_GROUP] = a.astype(BF16)
        ab_ref[g, :, FOURIER_GROUP:] = b.astype(BF16)


def _fourier_weights(w_f):
    return pl.pallas_call(
        _fourier_weight_kernel,
        out_shape=jax.ShapeDtypeStruct((N_FGROUPS, FOURIER_GROUP, 2 * FOURIER_GROUP), BF16),
        name="fourier_weights",
    )(jnp.asarray(_dft128_tables()), w_f)


def _inproj_kernel(x_ref, g_ref, w_ref, ab_ref, pq_ref, qt_ref, k_ref, vt_ref):
    x = x_ref[...]
    ms = jnp.mean(x * x, axis=-1, keepdims=True)
    h = (x * lax.rsqrt(ms + EPS) * g_ref[...]).astype(BF16)

    def proj(col):
        return jnp.dot(h, w_ref[:, col:col + ATTN_WIDTH], preferred_element_type=F32)

    ub = proj(0).astype(BF16)
    for g in range(N_FGROUPS):
        lo, hi = g * FOURIER_GROUP, (g + 1) * FOURIER_GROUP
        pq = jnp.dot(ub[:, lo:hi], ab_ref[g], preferred_element_type=F32)
        pq_ref[0, :, lo:hi] = pq[:, :FOURIER_GROUP].astype(BF16)
        pq_ref[1, :, lo:hi] = pq[:, FOURIER_GROUP:].astype(BF16)
    qt_ref[0] = (proj(FOURIER_WIDTH) * (HALF_DIM ** -0.5 * LOG2E)).T.astype(BF16)
    k_ref[...] = proj(FOURIER_WIDTH + ATTN_WIDTH).astype(BF16)
    vt_ref[0] = proj(FOURIER_WIDTH + 2 * ATTN_WIDTH).T.astype(BF16)


def _inproj(x2d, g, w_in_bf, ab):
    t = x2d.shape[0]
    tm = TOKEN_TILE
    nt = t // tm
    return pl.pallas_call(
        _inproj_kernel,
        grid=(nt,),
        in_specs=[
            pl.BlockSpec((tm, D_MODEL), lambda i: (i, 0)),
            pl.BlockSpec((1, D_MODEL), lambda i: (0, 0)),
            pl.BlockSpec((D_MODEL, FOURIER_WIDTH + 3 * ATTN_WIDTH), lambda i: (0, 0), pipeline_mode=pl.Buffered(1)),
            pl.BlockSpec((N_FGROUPS, FOURIER_GROUP, 2 * FOURIER_GROUP), lambda i: (0, 0, 0)),
        ],
        out_specs=[
            pl.BlockSpec((2, tm, FOURIER_WIDTH), lambda i: (0, i, 0)),
            pl.BlockSpec((1, ATTN_WIDTH, tm), lambda i: (i, 0, 0)),
            pl.BlockSpec((tm, ATTN_WIDTH), lambda i: (i, 0)),
            pl.BlockSpec((1, ATTN_WIDTH, tm), lambda i: (i, 0, 0)),
        ],
        out_shape=[
            jax.ShapeDtypeStruct((2, t, FOURIER_WIDTH), BF16),
            jax.ShapeDtypeStruct((nt, ATTN_WIDTH, tm), BF16),
            jax.ShapeDtypeStruct((t, ATTN_WIDTH), BF16),
            jax.ShapeDtypeStruct((nt, ATTN_WIDTH, tm), BF16),
        ],
        compiler_params=_cparams(("arbitrary",)),
        name="inproj",
    )(x2d, g.reshape(1, D_MODEL), w_in_bf, ab)


def _fourier1_kernel(x_ref, d_ref, o_ref, *, n1, rows):
    x = pltpu.einshape("abc->bac", x_ref[...].reshape(2 * n1, rows, FOURIER_WIDTH))
    d = d_ref[...]
    for r in range(rows):
        y = jnp.dot(d, x[r], preferred_element_type=F32)
        o_ref[0, r] = y[:n1].astype(BF16)
        o_ref[1, r] = y[n1:].astype(BF16)


def _fourier1(pq5, n1):
    b = pq5.shape[1]
    rows = 16 if n1 >= 64 else 64
    d1 = jnp.asarray(_stage1_matrix(n1)).astype(BF16)
    return pl.pallas_call(
        functools.partial(_fourier1_kernel, n1=n1, rows=rows),
        grid=(b, SEQ_MINOR // rows),
        in_specs=[
            pl.BlockSpec((2, None, n1, rows, FOURIER_WIDTH), lambda i, j: (0, i, 0, j, 0)),
            pl.BlockSpec((2 * n1, 2 * n1), lambda i, j: (0, 0)),
        ],
        out_specs=pl.BlockSpec((None, 2, rows, n1, FOURIER_WIDTH), lambda i, j: (i, 0, j, 0, 0)),
        out_shape=jax.ShapeDtypeStruct((b, 2, SEQ_MINOR, n1, FOURIER_WIDTH), BF16),
        compiler_params=_cparams(("arbitrary", "arbitrary")),
        name="fourier_stage1",
    )(pq5, d1)


def _fourier2_kernel(x_ref, m_ref, o_ref, *, rows):
    x = pltpu.einshape("abc->bac", x_ref[...].reshape(2 * SEQ_MINOR, rows, FOURIER_WIDTH))
    ys = [jnp.dot(m_ref[i], x[i], preferred_element_type=F32).astype(BF16) for i in range(rows)]
    o_ref[...] = pltpu.einshape("abc->bac", jnp.stack(ys))


def _fourier2(y1, n1):
    b = y1.shape[0]
    rows = 16
    m2 = jnp.asarray(_stage2_matrices(n1)).astype(BF16)
    return pl.pallas_call(
        functools.partial(_fourier2_kernel, rows=rows),
        grid=(b, n1 // rows),
        in_specs=[
            pl.BlockSpec((None, 2, SEQ_MINOR, rows, FOURIER_WIDTH), lambda i, j: (i, 0, 0, j, 0)),
            pl.BlockSpec((rows, SEQ_MINOR, 2 * SEQ_MINOR), lambda i, j: (j, 0, 0)),
        ],
        out_specs=pl.BlockSpec((None, SEQ_MINOR, rows, FOURIER_WIDTH), lambda i, j: (i, 0, j, 0)),
        out_shape=jax.ShapeDtypeStruct((b, SEQ_MINOR, n1, FOURIER_WIDTH), BF16),
        compiler_params=_cparams(("arbitrary", "arbitrary")),
        name="fourier_stage2",
    )(y1, m2)


def _t5_bucket(rel):
    nb = NUM_BUCKETS // 2
    max_exact = nb // 2
    ret = (rel > 0).astype(jnp.int32) * nb
    n = jnp.abs(rel)
    nf = jnp.maximum(n, 1).astype(jnp.float32)
    large = max_exact + (jnp.log(nf / max_exact) / math.log(MAX_DISTANCE / max_exact)
                         * (nb - max_exact)).astype(jnp.int32)
    large = jnp.minimum(large, nb - 1)
    return ret + jnp.where(n < max_exact, n, large)


def _bias_tables(rel_bias):
    tq, w = max(ATTN_Q_TILE, ATTN_Q_TILE_SHORT), ATTN_LANE_CHUNK
    mp = np.arange(2 * w)
    m = np.where(mp < w, mp, mp - 2 * w)
    d = (np.arange(3)[:, None] - 1) * w - m[None, :]
    tbl = rel_bias.astype(F32) * LOG2E
    g = tbl[_t5_bucket(jnp.asarray(d, jnp.int32))]
    wtab = jnp.transpose(g, (2, 0, 1))[:, :, None, :]
    far = tbl[_t5_bucket(jnp.asarray([-2 * MAX_DISTANCE, 2 * MAX_DISTANCE], jnp.int32))]
    cfar = jnp.broadcast_to(far.T[:, :, None, None], (N_HEADS, 2, 1, tq))
    return wtab, cfar


def _attn_kernel(lam_ref, qt_ref, k_ref, vt_ref, w_ref, cfar_ref, g_ref, o_ref, m_sc, l_sc, acc_sc, bias_sc, ex_sc,
                 *, nkv, tq, tk):
    qi = pl.program_id(2)
    r = tq // tk
    w = ATTN_LANE_CHUNK
    chunks = tq // w
    pieces = tk // TOKEN_TILE

    @pl.when((pl.program_id(1) == 0) & (qi == 0))
    def _():
        for u in range(3):
            x = jnp.broadcast_to(w_ref[u], (w, 2 * w))
            bias_sc[u] = pltpu.roll(x, 0, 1, stride=1, stride_axis=0)[:, :w]

    def sub_bias(tile, a, lc):
        dist = (tile - 1) * tk + (a - lc) * w
        if abs(dist) >= w - 1 + MAX_DISTANCE:
            return None, cfar_ref[0 if dist < 0 else 1][:, lc * w:(lc + 1) * w]
        return dist // w + 1, None

    row = lax.broadcasted_iota(jnp.int32, (ATTN_HEAD, w), 0)
    qts = []
    for lc in range(chunks):
        q = qt_ref[(lc * w) // TOKEN_TILE][:, (lc * w) % TOKEN_TILE:(lc * w) % TOKEN_TILE + w]
        zero = jnp.zeros_like(q)
        qts.append((jnp.where(row < HALF_DIM, q, zero), jnp.where(row >= HALF_DIM, q, zero)))

    lo = jnp.maximum(qi * r - 1, 0)
    hi = jnp.minimum((qi + 1) * r + 1, nkv)

    def logits(j, c, lc):
        kj = k_ref[pl.ds(pl.multiple_of(j * tk, tk), tk), :]
        return jnp.dot(kj, qts[lc][c], preferred_element_type=F32)

    def pv(j, p):
        pb = p.astype(BF16)
        out = jnp.dot(vt_ref[j * pieces], pb[:TOKEN_TILE], preferred_element_type=F32)
        for i in range(1, pieces):
            out += jnp.dot(vt_ref[j * pieces + i], pb[i * TOKEN_TILE:(i + 1) * TOKEN_TILE],
                           preferred_element_type=F32)
        return out

    def safe_block(j, tile, c_row):
        ss = [[logits(j, c, lc) for lc in range(chunks)] for c in range(2)]
        for c in range(2):
            for lc in range(chunks):
                sl = slice(lc * w, (lc + 1) * w)
                s = ss[c][lc]
                m_old = m_sc[c, :, sl]
                if tile is not None:
                    ts = []
                    for a in range(tk // w):
                        u, side = sub_bias(tile, a, lc)
                        sa = jnp.maximum(s[a * w:(a + 1) * w], NEG_BIG)
                        ts.append(sa + (side if u is None else bias_sc[u]))
                    s = jnp.concatenate(ts, axis=0)
                    m_new = jnp.maximum(m_old, jnp.max(s, axis=0, keepdims=True))
                    p = jnp.exp2(s - m_new)
                else:
                    cr = c_row[:, sl]
                    m_new = jnp.maximum(m_old, jnp.max(s, axis=0, keepdims=True) + cr)
                    p = jnp.exp2(s - (m_new - cr))
                alpha = jnp.exp2(m_old - m_new)
                l_sc[c, :, sl] = alpha * l_sc[c, :, sl] + jnp.sum(p, axis=0, keepdims=True)
                acc_sc[c, :, sl] = alpha * acc_sc[c, :, sl] + pv(j, p)
                m_sc[c, :, sl] = m_new

    def fast_block(j, tile, c_row):
        order = [(c, lc) for c in range(2) for lc in range(chunks)]
        ahead = len(order) if tile is None else ATTN_QK_AHEAD_NEAR
        pending = {key: logits(j, *key) for key in order[:ahead]}
        for idx, (c, lc) in enumerate(order):
            if idx + ahead < len(order):
                nxt = order[idx + ahead]
                pending[nxt] = logits(j, *nxt)
            sl = slice(lc * w, (lc + 1) * w)
            s = pending.pop((c, lc))
            m_old = m_sc[c, :, sl]
            if tile is not None:
                es = []
                for a in range(tk // w):
                    rows = slice(a * w, (a + 1) * w)
                    u, side = sub_bias(tile, a, lc)
                    if u is None:
                        es.append(s[rows] - (m_old - side))
                    else:
                        es.append((s[rows] - m_old) + bias_sc[u])
                e = jnp.concatenate(es, axis=0)
                p = jnp.exp2(e)
                bm = jnp.max(e, axis=0, keepdims=True) + m_old
            else:
                cr = c_row[:, sl]
                p = jnp.exp2(s - (m_old - cr))
                bm = jnp.max(s, axis=0, keepdims=True) + cr
            m_new = jnp.maximum(m_old, bm)
            alpha = jnp.exp2(m_old - m_new)
            ex_sc[c, :, sl] = jnp.maximum(ex_sc[c, :, sl], bm - m_old)
            l_sc[c, :, sl] = alpha * (l_sc[c, :, sl] + jnp.sum(p, axis=0, keepdims=True))
            acc_sc[c, :, sl] = alpha * (acc_sc[c, :, sl] + pv(j, p))
            m_sc[c, :, sl] = m_new

    def run(block_fn, pair_far):
        def far(c_row):
            def body(j, carry):
                block_fn(j, None, c_row)
                return carry
            return body

        def near():
            for t in range(r + 2):
                j_t = qi * r - 1 + t

                @pl.when((j_t >= 0) & (j_t < nkv))
                def _():
                    block_fn(j_t, t, None)

        def far_pair(c_row):
            def body(i, first):
                block_fn(first + 2 * i, None, c_row)
                block_fn(first + 2 * i + 1, None, c_row)
                return first
            return body

        if pair_far:
            n_lo = lo // 2
            lax.fori_loop(0, n_lo, far_pair(cfar_ref[0]), 0)
            lax.fori_loop(2 * n_lo, lo, far(cfar_ref[0]), 0)
            near()
            n_hi = (nkv - hi) // 2
            lax.fori_loop(0, n_hi, far_pair(cfar_ref[1]), hi)
            lax.fori_loop(hi + 2 * n_hi, nkv, far(cfar_ref[1]), 0)
        else:
            lax.fori_loop(0, lo, far(cfar_ref[0]), 0)
            near()
            lax.fori_loop(hi, nkv, far(cfar_ref[1]), 0)

    l_sc[...] = jnp.zeros(l_sc.shape, F32)
    acc_sc[...] = jnp.zeros(acc_sc.shape, F32)
    ex_sc[...] = jnp.zeros(ex_sc.shape, F32)

    k0 = k_ref[0:ATTN_LEVEL_ROWS, :]

    @pl.when(qi == 0)
    def _():
        for c in range(2):
            for lc in range(chunks):
                sl = slice(lc * w, (lc + 1) * w)
                s0 = jnp.maximum(jnp.dot(k0, qts[lc][c], preferred_element_type=F32), NEG_BIG)
                u, side = sub_bias(1, 0, lc)
                s0 = s0 + (side if u is None else bias_sc[u, 0:ATTN_LEVEL_ROWS, :])
                m_sc[c, :, sl] = jnp.max(s0, axis=0, keepdims=True)

    @pl.when(qi != 0)
    def _():
        for c in range(2):
            for lc in range(chunks):
                sl = slice(lc * w, (lc + 1) * w)
                s0 = jnp.dot(k0, qts[lc][c], preferred_element_type=F32)
                m_sc[c, :, sl] = jnp.max(s0, axis=0, keepdims=True) + cfar_ref[0][:, sl]

    run(fast_block, True)

    @pl.when(jnp.max(ex_sc[...]) > ATTN_EXP_GUARD)
    def _():
        m_sc[...] = jnp.full(m_sc.shape, NEG_BIG, F32)
        l_sc[...] = jnp.zeros(l_sc.shape, F32)
        acc_sc[...] = jnp.zeros(acc_sc.shape, F32)
        run(safe_block, False)

    lam = (jnp.exp(jnp.sum(lam_ref[0:1, :] * lam_ref[1:2, :], axis=-1, keepdims=True))
           - jnp.exp(jnp.sum(lam_ref[2:3, :] * lam_ref[3:4, :], axis=-1, keepdims=True))
           + LAM_INIT)
    o = acc_sc[0] / l_sc[0] - lam * (acc_sc[1] / l_sc[1])
    ms = jnp.mean(o * o, axis=0, keepdims=True)
    o = o * lax.rsqrt(ms + EPS) * g_ref[...]
    o = o * (1.0 - LAM_INIT)
    o_ref[...] = o.T.astype(BF16)


def _attention(qt, k, vt, wtab, cfar, lam_params, subln_g, b, s):
    tq = ATTN_Q_TILE_SHORT if s <= ATTN_Q_TILE_SHORT else ATTN_Q_TILE
    tk, w = ATTN_KV_TILE, ATTN_LANE_CHUNK
    assert s % tq == 0 and s % tk == 0 and tq % tk == 0 and tk % TOKEN_TILE == 0
    nq = s // tq
    nkv = s // tk
    t = b * s
    return pl.pallas_call(
        functools.partial(_attn_kernel, nkv=nkv, tq=tq, tk=tk),
        grid=(N_HEADS, b, nq),
        in_specs=[
            pl.BlockSpec((4, HALF_DIM), lambda h, bi, qi: (0, 0)),
            pl.BlockSpec((tq // TOKEN_TILE, ATTN_HEAD, TOKEN_TILE), lambda h, bi, qi: (bi * nq + qi, h, 0)),
            pl.BlockSpec((s, ATTN_HEAD), lambda h, bi, qi: (bi, h)),
            pl.BlockSpec((s // TOKEN_TILE, ATTN_HEAD, TOKEN_TILE), lambda h, bi, qi: (bi, h, 0)),
            pl.BlockSpec((None, 3, 1, 2 * w), lambda h, bi, qi: (h, 0, 0, 0)),
            pl.BlockSpec((None, 2, 1, tq), lambda h, bi, qi: (h, 0, 0, 0)),
            pl.BlockSpec((ATTN_HEAD, 1), lambda h, bi, qi: (0, 0)),
        ],
        out_specs=pl.BlockSpec((tq, ATTN_HEAD), lambda h, bi, qi: (bi * nq + qi, h)),
        out_shape=jax.ShapeDtypeStruct((t, ATTN_WIDTH), BF16),
        scratch_shapes=[
            pltpu.VMEM((2, 1, tq), F32),
            pltpu.VMEM((2, 1, tq), F32),
            pltpu.VMEM((2, ATTN_HEAD, tq), F32),
            pltpu.VMEM((3, w, w), F32),
            pltpu.VMEM((2, 1, tq), F32),
        ],
        compiler_params=_cparams(("arbitrary", "arbitrary", "arbitrary")),
        name="diff_attention",
    )(lam_params, qt, k, vt, wtab, cfar, subln_g.reshape(ATTN_HEAD, 1))


def _outproj_kernel(x_ref, yf_ref, ya_ref, w_ref, g_ref, x1_ref, h2_ref):
    y = (jnp.dot(yf_ref[...], w_ref[:FOURIER_WIDTH, :], preferred_element_type=F32)
         + jnp.dot(ya_ref[...], w_ref[FOURIER_WIDTH:, :], preferred_element_type=F32))
    x1 = x_ref[...] + y
    x1_ref[...] = x1
    ms = jnp.mean(x1 * x1, axis=-1, keepdims=True)
    h2_ref[...] = (x1 * lax.rsqrt(ms + EPS) * g_ref[...]).astype(BF16)


def _outproj(x2d, yf, ya, w_out_bf, g2):
    t = x2d.shape[0]
    tm = TOKEN_TILE
    return pl.pallas_call(
        _outproj_kernel,
        grid=(t // tm,),
        in_specs=[
            pl.BlockSpec((tm, D_MODEL), lambda i: (i, 0)),
            pl.BlockSpec((tm, FOURIER_WIDTH), lambda i: (i, 0)),
            pl.BlockSpec((tm, ATTN_WIDTH), lambda i: (i, 0)),
            pl.BlockSpec((D_MODEL, D_MODEL), lambda i: (0, 0)),
            pl.BlockSpec((1, D_MODEL), lambda i: (0, 0)),
        ],
        out_specs=[
            pl.BlockSpec((tm, D_MODEL), lambda i: (i, 0)),
            pl.BlockSpec((tm, D_MODEL), lambda i: (i, 0)),
        ],
        out_shape=[
            jax.ShapeDtypeStruct((t, D_MODEL), F32),
            jax.ShapeDtypeStruct((t, D_MODEL), BF16),
        ],
        compiler_params=_cparams(("arbitrary",)),
        name="outproj",
    )(x2d, yf, ya, w_out_bf, g2.reshape(1, D_MODEL))


def _ffn_kernel(h_ref, x1_ref, wg_ref, wu_ref, wd_ref, g_ref, o_ref, acc_ref):
    f = pl.program_id(1)

    @pl.when(f == 0)
    def _():
        acc_ref[...] = jnp.zeros(acc_ref.shape, F32)

    h = h_ref[...]
    gate = jnp.dot(h, wg_ref[...], preferred_element_type=F32)
    up = jnp.dot(h, wu_ref[...], preferred_element_type=F32)
    a = (gate * (1.0 / (1.0 + jnp.exp(-gate))) * up).astype(BF16)
    acc_ref[...] += jnp.dot(a, wd_ref[...], preferred_element_type=F32)

    @pl.when(f == pl.num_programs(1) - 1)
    def _():
        x2 = x1_ref[...] + acc_ref[...]
        ms = jnp.mean(x2 * x2, axis=-1, keepdims=True)
        o_ref[...] = x2 * lax.rsqrt(ms + EPS) * g_ref[...]


def _ffn(h2, x1, wg_bf, wu_bf, wd_bf, gf):
    t = h2.shape[0]
    tm = TOKEN_TILE
    tf = FF_TILE
    return pl.pallas_call(
        _ffn_kernel,
        grid=(t // tm, D_FF // tf),
        in_specs=[
            pl.BlockSpec((tm, D_MODEL), lambda i, f: (i, 0)),
            pl.BlockSpec((tm, D_MODEL), lambda i, f: (i, 0)),
            pl.BlockSpec((D_MODEL, tf), lambda i, f: (0, f)),
            pl.BlockSpec((D_MODEL, tf), lambda i, f: (0, f)),
            pl.BlockSpec((tf, D_MODEL), lambda i, f: (f, 0)),
            pl.BlockSpec((1, D_MODEL), lambda i, f: (0, 0)),
        ],
        out_specs=pl.BlockSpec((tm, D_MODEL), lambda i, f: (i, 0)),
        out_shape=jax.ShapeDtypeStruct((t, D_MODEL), F32),
        scratch_shapes=[pltpu.VMEM((tm, D_MODEL), F32)],
        compiler_params=_cparams(("arbitrary", "arbitrary")),
        name="ffn",
    )(h2, x1, wg_bf, wu_bf, wd_bf, gf.reshape(1, D_MODEL))


def _encoder(x, params):
    b, s, _ = x.shape
    t = b * s
    n1 = s // SEQ_MINOR
    x2d = x.reshape(t, D_MODEL)

    pq, qt, k, vt = _inproj(x2d, params["norm1_g"], params["w_in"], params["ab"])
    y1 = _fourier1(pq.reshape(2, b, n1, SEQ_MINOR, FOURIER_WIDTH), n1)
    yf = _fourier2(y1, n1).reshape(t, FOURIER_WIDTH)
    ya = _attention(qt, k, vt, params["wtab"], params["cfar"], params["lam"], params["subln_g"], b, s)
    x1, h2 = _outproj(x2d, yf, ya, params["w_out"], params["norm2_g"])
    out = _ffn(h2, x1, params["w_gate"], params["w_up"], params["w_down"], params["final_g"])
    return out.reshape(b, s, D_MODEL)


def kernel(x_prompt, x_sample, norm1_g, w_in, w_fourier, lambda_q1, lambda_k1, lambda_q2, lambda_k2,
           subln_g, w_out, norm2_g, w_gate, w_up, w_down, rel_bias, final_g):
    wtab, cfar = _bias_tables(rel_bias)
    params = {
        "norm1_g": norm1_g[0].astype(F32),
        "w_in": w_in[0].astype(BF16),
        "ab": _fourier_weights(w_fourier[0].astype(F32)),
        "lam": jnp.stack([lambda_q1[0], lambda_k1[0], lambda_q2[0], lambda_k2[0]]).astype(F32),
        "subln_g": subln_g[0].astype(F32),
        "w_out": w_out[0].astype(BF16),
        "norm2_g": norm2_g[0].astype(F32),
        "w_gate": w_gate[0].astype(BF16),
        "w_up": w_up[0].astype(BF16),
        "w_down": w_down[0].astype(BF16),
        "wtab": wtab,
        "cfar": cfar,
        "final_g": final_g.astype(F32),
    }
    return (_encoder(x_prompt, params), _encoder(x_sample, params))
```

```python
import functools
import math

import numpy as np
import jax
import jax.numpy as jnp
from jax import lax
from jax.experimental import pallas as pl
from jax.experimental.pallas import tpu as pltpu

F32 = jnp.float32
BF16 = jnp.bfloat16

D_MODEL = 2048
FOURIER_WIDTH = 1024
FOURIER_GROUP = 128
N_FGROUPS = 8
ATTN_WIDTH = 1024
ATTN_HEAD = 128
N_HEADS = 8
HALF_DIM = 64
D_FF = 5632
NUM_BUCKETS = 32
MAX_DISTANCE = 128
EPS = 1e-6
LAM_INIT = 0.8 - 0.6 * math.exp(-0.3 * 0)

SEQ_MINOR = 128
TOKEN_TILE = 512
FF_TILE = 512
ATTN_Q_TILE = 1024
ATTN_Q_TILE_SHORT = 2048
ATTN_KV_TILE = 1024
ATTN_EXP_GUARD = 64.0
ATTN_LEVEL_ROWS = 16
ATTN_QK_AHEAD_NEAR = 4
ATTN_LANE_CHUNK = 256
LOG2E = math.log2(math.e)
NEG_BIG = -1e30
VMEM_LIMIT = 56 * 1024 * 1024


def _cparams(sem):
    return pltpu.CompilerParams(dimension_semantics=sem, vmem_limit_bytes=VMEM_LIMIT)


def _dft128_tables():
    n = np.arange(FOURIER_GROUP)
    ang = 2.0 * np.pi * ((n[:, None] * n[None, :]) % FOURIER_GROUP) / FOURIER_GROUP
    return np.stack([np.cos(ang), np.sin(ang)]).astype(np.float32)


def _stage1_matrix(n1):
    n = np.arange(n1)
    ang = 2.0 * np.pi * ((n[:, None] * n[None, :]) % n1) / n1
    c, s = np.cos(ang), np.sin(ang)
    return np.block([[c, -s], [-s, -c]]).astype(np.float32)


def _stage2_matrices(n1):
    s_len = n1 * SEQ_MINOR
    s1p = np.arange(n1)[:, None, None]
    s2p = np.arange(SEQ_MINOR)[None, :, None]
    s2 = np.arange(SEQ_MINOR)[None, None, :]
    ang = 2.0 * np.pi * ((s2 * (s1p + n1 * s2p)) % s_len) / s_len
    scale = 1.0 / math.sqrt(s_len * FOURIER_GROUP)
    return (np.concatenate([np.cos(ang), np.sin(ang)], axis=-1) * scale).astype(np.float32)


def _fourier_weight_kernel(dft_ref, w_ref, ab_ref):
    for g in range(N_FGROUPS):
        w = w_ref[g]
        a = jnp.dot(dft_ref[0], w, preferred_element_type=F32, precision=lax.Precision.HIGHEST)
        b = jnp.dot(dft_ref[1], w, preferred_element_type=F32, precision=lax.Precision.HIGHEST)
        ab_ref[g, :, :FOURIER_GROUP] = a.astype(BF16)
        ab_ref[g, :, FOURIER_GROUP:] = b.astype(BF16)


def _fourier_weights(w_f):
    return pl.pallas_call(
        _fourier_weight_kernel,
        out_shape=jax.ShapeDtypeStruct((N_FGROUPS, FOURIER_GROUP, 2 * FOURIER_GROUP), BF16),
        name="fourier_weights",
    )(jnp.asarray(_dft128_tables()), w_f)


def _inproj_kernel(x_ref, g_ref, w_ref, ab_ref, pq_ref, qt_ref, k_ref, vt_ref):
    x = x_ref[...]
    ms = jnp.mean(x * x, axis=-1, keepdims=True)
    h = (x * lax.rsqrt(ms + EPS) * g_ref[...]).astype(BF16)

    def proj(col):
        return jnp.dot(h, w_ref[:, col:col + ATTN_WIDTH], preferred_element_type=F32)

    ub = proj(0).astype(BF16)
    for g in range(N_FGROUPS):
        lo, hi = g * FOURIER_GROUP, (g + 1) * FOURIER_GROUP
        pq = jnp.dot(ub[:, lo:hi], ab_ref[g], preferred_element_type=F32)
        pq_ref[0, :, lo:hi] = pq[:, :FOURIER_GROUP].astype(BF16)
        pq_ref[1, :, lo:hi] = pq[:, FOURIER_GROUP:].astype(BF16)
    qt_ref[0] = (proj(FOURIER_WIDTH) * (HALF_DIM ** -0.5 * LOG2E)).T.astype(BF16)
    k_ref[...] = proj(FOURIER_WIDTH + ATTN_WIDTH).astype(BF16)
    vt_ref[0] = proj(FOURIER_WIDTH + 2 * ATTN_WIDTH).T.astype(BF16)


def _inproj(x2d, g, w_in_bf, ab):
    t = x2d.shape[0]
    tm = TOKEN_TILE
    nt = t // tm
    return pl.pallas_call(
        _inproj_kernel,
        grid=(nt,),
        in_specs=[
            pl.BlockSpec((tm, D_MODEL), lambda i: (i, 0)),
            pl.BlockSpec((1, D_MODEL), lambda i: (0, 0)),
            pl.BlockSpec((D_MODEL, FOURIER_WIDTH + 3 * ATTN_WIDTH), lambda i: (0, 0), pipeline_mode=pl.Buffered(1)),
            pl.BlockSpec((N_FGROUPS, FOURIER_GROUP, 2 * FOURIER_GROUP), lambda i: (0, 0, 0)),
        ],
        out_specs=[
            pl.BlockSpec((2, tm, FOURIER_WIDTH), lambda i: (0, i, 0)),
            pl.BlockSpec((1, ATTN_WIDTH, tm), lambda i: (i, 0, 0)),
            pl.BlockSpec((tm, ATTN_WIDTH), lambda i: (i, 0)),
            pl.BlockSpec((1, ATTN_WIDTH, tm), lambda i: (i, 0, 0)),
        ],
        out_shape=[
            jax.ShapeDtypeStruct((2, t, FOURIER_WIDTH), BF16),
            jax.ShapeDtypeStruct((nt, ATTN_WIDTH, tm), BF16),
            jax.ShapeDtypeStruct((t, ATTN_WIDTH), BF16),
            jax.ShapeDtypeStruct((nt, ATTN_WIDTH, tm), BF16),
        ],
        compiler_params=_cparams(("arbitrary",)),
        name="inproj",
    )(x2d, g.reshape(1, D_MODEL), w_in_bf, ab)


def _fourier1_kernel(x_ref, d_ref, o_ref, *, n1, rows):
    x = pltpu.einshape("abc->bac", x_ref[...].reshape(2 * n1, rows, FOURIER_WIDTH))
    d = d_ref[...]
    for r in range(rows):
        y = jnp.dot(d, x[r], preferred_element_type=F32)
        o_ref[0, r] = y[:n1].astype(BF16)
        o_ref[1, r] = y[n1:].astype(BF16)


def _fourier1(pq5, n1):
    b = pq5.shape[1]
    rows = 16 if n1 >= 64 else 64
    d1 = jnp.asarray(_stage1_matrix(n1)).astype(BF16)
    return pl.pallas_call(
        functools.partial(_fourier1_kernel, n1=n1, rows=rows),
        grid=(b, SEQ_MINOR // rows),
        in_specs=[
            pl.BlockSpec((2, None, n1, rows, FOURIER_WIDTH), lambda i, j: (0, i, 0, j, 0)),
            pl.BlockSpec((2 * n1, 2 * n1), lambda i, j: (0, 0)),
        ],
        out_specs=pl.BlockSpec((None, 2, rows, n1, FOURIER_WIDTH), lambda i, j: (i, 0, j, 0, 0)),
        out_shape=jax.ShapeDtypeStruct((b, 2, SEQ_MINOR, n1, FOURIER_WIDTH), BF16),
        compiler_params=_cparams(("arbitrary", "arbitrary")),
        name="fourier_stage1",
    )(pq5, d1)


def _fourier2_kernel(x_ref, m_ref, o_ref, *, rows):
    x = pltpu.einshape("abc->bac", x_ref[...].reshape(2 * SEQ_MINOR, rows, FOURIER_WIDTH))
    ys = [jnp.dot(m_ref[i], x[i], preferred_element_type=F32).astype(BF16) for i in range(rows)]
    o_ref[...] = pltpu.einshape("abc->bac", jnp.stack(ys))


def _fourier2(y1, n1):
    b = y1.shape[0]
    rows = 16
    m2 = jnp.asarray(_stage2_matrices(n1)).astype(BF16)
    return pl.pallas_call(
        functools.partial(_fourier2_kernel, rows=rows),
        grid=(b, n1 // rows),
        in_specs=[
            pl.BlockSpec((None, 2, SEQ_MINOR, rows, FOURIER_WIDTH), lambda i, j: (i, 0, 0, j, 0)),
            pl.BlockSpec((rows, SEQ_MINOR, 2 * SEQ_MINOR), lambda i, j: (j, 0, 0)),
        ],
        out_specs=pl.BlockSpec((None, SEQ_MINOR, rows, FOURIER_WIDTH), lambda i, j: (i, 0, j, 0)),
        out_shape=jax.ShapeDtypeStruct((b, SEQ_MINOR, n1, FOURIER_WIDTH), BF16),
        compiler_params=_cparams(("arbitrary", "arbitrary")),
        name="fourier_stage2",
    )(y1, m2)


def _t5_bucket(rel):
    nb = NUM_BUCKETS // 2
    max_exact = nb // 2
    ret = (rel > 0).astype(jnp.int32) * nb
    n = jnp.abs(rel)
    nf = jnp.maximum(n, 1).astype(jnp.float32)
    large = max_exact + (jnp.log(nf / max_exact) / math.log(MAX_DISTANCE / max_exact)
                         * (nb - max_exact)).astype(jnp.int32)
    large = jnp.minimum(large, nb - 1)
    return ret + jnp.where(n < max_exact, n, large)


def _bias_tables(rel_bias):
    tq, w = max(ATTN_Q_TILE, ATTN_Q_TILE_SHORT), ATTN_LANE_CHUNK
    mp = np.arange(2 * w)
    m = np.where(mp < w, mp, mp - 2 * w)
    d = (np.arange(3)[:, None] - 1) * w - m[None, :]
    tbl = rel_bias.astype(F32) * LOG2E
    g = tbl[_t5_bucket(jnp.asarray(d, jnp.int32))]
    wtab = jnp.transpose(g, (2, 0, 1))[:, :, None, :]
    far = tbl[_t5_bucket(jnp.asarray([-2 * MAX_DISTANCE, 2 * MAX_DISTANCE], jnp.int32))]
    cfar = jnp.broadcast_to(far.T[:, :, None, None], (N_HEADS, 2, 1, tq))
    return wtab, cfar


def _attn_kernel(lam_ref, qt_ref, k_ref, vt_ref, w_ref, cfar_ref, g_ref, o_ref, m_sc, l_sc, acc_sc, bias_sc, ex_sc,
                 *, nkv, tq, tk):
    qi = pl.program_id(2)
    r = tq // tk
    w = ATTN_LANE_CHUNK
    chunks = tq // w
    pieces = tk // TOKEN_TILE

    @pl.when((pl.program_id(1) == 0) & (qi == 0))
    def _():
        for u in range(3):
            x = jnp.broadcast_to(w_ref[u], (w, 2 * w))
            bias_sc[u] = pltpu.roll(x, 0, 1, stride=1, stride_axis=0)[:, :w]

    def sub_bias(tile, a, lc):
        dist = (tile - 1) * tk + (a - lc) * w
        if abs(dist) >= w - 1 + MAX_DISTANCE:
            return None, cfar_ref[0 if dist < 0 else 1][:, lc * w:(lc + 1) * w]
        return dist // w + 1, None

    row = lax.broadcasted_iota(jnp.int32, (ATTN_HEAD, w), 0)
    qts = []
    for lc in range(chunks):
        q = qt_ref[(lc * w) // TOKEN_TILE][:, (lc * w) % TOKEN_TILE:(lc * w) % TOKEN_TILE + w]
        zero = jnp.zeros_like(q)
        qts.append((jnp.where(row < HALF_DIM, q, zero), jnp.where(row >= HALF_DIM, q, zero)))

    lo = jnp.maximum(qi * r - 1, 0)
    hi = jnp.minimum((qi + 1) * r + 1, nkv)

    def logits(j, c, lc):
        kj = k_ref[pl.ds(pl.multiple_of(j * tk, tk), tk), :]
        return jnp.dot(kj, qts[lc][c], preferred_element_type=F32)

    def pv(j, p):
        pb = p.astype(BF16)
        out = jnp.dot(vt_ref[j * pieces], pb[:TOKEN_TILE], preferred_element_type=F32)
        for i in range(1, pieces):
            out += jnp.dot(vt_ref[j * pieces + i], pb[i * TOKEN_TILE:(i + 1) * TOKEN_TILE],
                           preferred_element_type=F32)
        return out

    def safe_block(j, tile, c_row):
        ss = [[logits(j, c, lc) for lc in range(chunks)] for c in range(2)]
        for c in range(2):
            for lc in range(chunks):
                sl = slice(lc * w, (lc + 1) * w)
                s = ss[c][lc]
                m_old = m_sc[c, :, sl]
                if tile is not None:
                    ts = []
                    for a in range(tk // w):
                        u, side = sub_bias(tile, a, lc)
                        sa = jnp.maximum(s[a * w:(a + 1) * w], NEG_BIG)
                        ts.append(sa + (side if u is None else bias_sc[u]))
                    s = jnp.concatenate(ts, axis=0)
                    m_new = jnp.maximum(m_old, jnp.max(s, axis=0, keepdims=True))
                    p = jnp.exp2(s - m_new)
                else:
                    cr = c_row[:, sl]
                    m_new = jnp.maximum(m_old, jnp.max(s, axis=0, keepdims=True) + cr)
                    p = jnp.exp2(s - (m_new - cr))
                alpha = jnp.exp2(m_old - m_new)
                l_sc[c, :, sl] = alpha * l_sc[c, :, sl] + jnp.sum(p, axis=0, keepdims=True)
                acc_sc[c, :, sl] = alpha * acc_sc[c, :, sl] + pv(j, p)
                m_sc[c, :, sl] = m_new

    def fast_block(j, tile, c_row):
        order = [(c, lc) for c in range(2) for lc in range(chunks)]
        ahead = len(order) if tile is None else ATTN_QK_AHEAD_NEAR
        pending = {key: logits(j, *key) for key in order[:ahead]}
        for idx, (c, lc) in enumerate(order):
            if idx + ahead < len(order):
                nxt = order[idx + ahead]
                pending[nxt] = logits(j, *nxt)
            sl = slice(lc * w, (lc + 1) * w)
            s = pending.pop((c, lc))
            m_old = m_sc[c, :, sl]
            if tile is not None:
                es = []
                for a in range(tk // w):
                    rows = slice(a * w, (a + 1) * w)
                    u, side = sub_bias(tile, a, lc)
                    if u is None:
                        es.append(s[rows] - (m_old - side))
                    else:
                        es.append((s[rows] - m_old) + bias_sc[u])
                e = jnp.concatenate(es, axis=0)
                p = jnp.exp2(e)
                bm = jnp.max(e, axis=0, keepdims=True) + m_old
            else:
                cr = c_row[:, sl]
                p = jnp.exp2(s - (m_old - cr))
                bm = jnp.max(s, axis=0, keepdims=True) + cr
            m_new = jnp.maximum(m_old, bm)
            alpha = jnp.exp2(m_old - m_new)
            ex_sc[c, :, sl] = jnp.maximum(ex_sc[c, :, sl], bm - m_old)
            l_sc[c, :, sl] = alpha * (l_sc[c, :, sl] + jnp.sum(p, axis=0, keepdims=True))
            acc_sc[c, :, sl] = alpha * (acc_sc[c, :, sl] + pv(j, p))
            m_sc[c, :, sl] = m_new

    def run(block_fn, pair_far):
        def far(c_row):
            def body(j, carry):
                block_fn(j, None, c_row)
                return carry
            return body

        def near():
            for t in range(r + 2):
                j_t = qi * r - 1 + t

                @pl.when((j_t >= 0) & (j_t < nkv))
                def _():
                    block_fn(j_t, t, None)

        def far_pair(c_row):
            def body(i, first):
                block_fn(first + 2 * i, None, c_row)
                block_fn(first + 2 * i + 1, None, c_row)
                return first
            return body

        if pair_far:
            n_lo = lo // 2
            lax.fori_loop(0, n_lo, far_pair(cfar_ref[0]), 0)
            lax.fori_loop(2 * n_lo, lo, far(cfar_ref[0]), 0)
            near()
            n_hi = (nkv - hi) // 2
            lax.fori_loop(0, n_hi, far_pair(cfar_ref[1]), hi)
            lax.fori_loop(hi + 2 * n_hi, nkv, far(cfar_ref[1]), 0)
        else:
            lax.fori_loop(0, lo, far(cfar_ref[0]), 0)
            near()
            lax.fori_loop(hi, nkv, far(cfar_ref[1]), 0)

    l_sc[...] = jnp.zeros(l_sc.shape, F32)
    acc_sc[...] = jnp.zeros(acc_sc.shape, F32)
    ex_sc[...] = jnp.zeros(ex_sc.shape, F32)

    k0 = k_ref[0:ATTN_LEVEL_ROWS, :]

    @pl.when(qi == 0)
    def _():
        for c in range(2):
            for lc in range(chunks):
                sl = slice(lc * w, (lc + 1) * w)
                s0 = jnp.maximum(jnp.dot(k0, qts[lc][c], preferred_element_type=F32), NEG_BIG)
                u, side = sub_bias(1, 0, lc)
                s0 = s0 + (side if u is None else bias_sc[u, 0:ATTN_LEVEL_ROWS, :])
                m_sc[c, :, sl] = jnp.max(s0, axis=0, keepdims=True)

    @pl.when(qi != 0)
    def _():
        for c in range(2):
            for lc in range(chunks):
                sl = slice(lc * w, (lc + 1) * w)
                s0 = jnp.dot(k0, qts[lc][c], preferred_element_type=F32)
                m_sc[c, :, sl] = jnp.max(s0, axis=0, keepdims=True) + cfar_ref[0][:, sl]

    run(fast_block, True)

    @pl.when(jnp.max(ex_sc[...]) > ATTN_EXP_GUARD)
    def _():
        m_sc[...] = jnp.full(m_sc.shape, NEG_BIG, F32)
        l_sc[...] = jnp.zeros(l_sc.shape, F32)
        acc_sc[...] = jnp.zeros(acc_sc.shape, F32)
        run(safe_block, False)

    lam = (jnp.exp(jnp.sum(lam_ref[0:1, :] * lam_ref[1:2, :], axis=-1, keepdims=True))
           - jnp.exp(jnp.sum(lam_ref[2:3, :] * lam_ref[3:4, :], axis=-1, keepdims=True))
           + LAM_INIT)
    o = acc_sc[0] / l_sc[0] - lam * (acc_sc[1] / l_sc[1])
    ms = jnp.mean(o * o, axis=0, keepdims=True)
    o = o * lax.rsqrt(ms + EPS) * g_ref[...]
    o = o * (1.0 - LAM_INIT)
    o_ref[...] = o.T.astype(BF16)


def _attention(qt, k, vt, wtab, cfar, lam_params, subln_g, b, s):
    tq = ATTN_Q_TILE_SHORT if s <= ATTN_Q_TILE_SHORT else ATTN_Q_TILE
    tk, w = ATTN_KV_TILE, ATTN_LANE_CHUNK
    assert s % tq == 0 and s % tk == 0 and tq % tk == 0 and tk % TOKEN_TILE == 0
    nq = s // tq
    nkv = s // tk
    t = b * s
    return pl.pallas_call(
        functools.partial(_attn_kernel, nkv=nkv, tq=tq, tk=tk),
        grid=(N_HEADS, b, nq),
        in_specs=[
            pl.BlockSpec((4, HALF_DIM), lambda h, bi, qi: (0, 0)),
            pl.BlockSpec((tq // TOKEN_TILE, ATTN_HEAD, TOKEN_TILE), lambda h, bi, qi: (bi * nq + qi, h, 0)),
            pl.BlockSpec((s, ATTN_HEAD), lambda h, bi, qi: (bi, h)),
            pl.BlockSpec((s // TOKEN_TILE, ATTN_HEAD, TOKEN_TILE), lambda h, bi, qi: (bi, h, 0)),
            pl.BlockSpec((None, 3, 1, 2 * w), lambda h, bi, qi: (h, 0, 0, 0)),
            pl.BlockSpec((None, 2, 1, tq), lambda h, bi, qi: (h, 0, 0, 0)),
            pl.BlockSpec((ATTN_HEAD, 1), lambda h, bi, qi: (0, 0)),
        ],
        out_specs=pl.BlockSpec((tq, ATTN_HEAD), lambda h, bi, qi: (bi * nq + qi, h)),
        out_shape=jax.ShapeDtypeStruct((t, ATTN_WIDTH), BF16),
        scratch_shapes=[
            pltpu.VMEM((2, 1, tq), F32),
            pltpu.VMEM((2, 1, tq), F32),
            pltpu.VMEM((2, ATTN_HEAD, tq), F32),
            pltpu.VMEM((3, w, w), F32),
            pltpu.VMEM((2, 1, tq), F32),
        ],
        compiler_params=_cparams(("arbitrary", "arbitrary", "arbitrary")),
        name="diff_attention",
    )(lam_params, qt, k, vt, wtab, cfar, subln_g.reshape(ATTN_HEAD, 1))


def _outproj_kernel(x_ref, yf_ref, ya_ref, w_ref, g_ref, x1_ref, h2_ref):
    y = (jnp.dot(yf_ref[...], w_ref[:FOURIER_WIDTH, :], preferred_element_type=F32)
         + jnp.dot(ya_ref[...], w_ref[FOURIER_WIDTH:, :], preferred_element_type=F32))
    x1 = x_ref[...] + y
    x1_ref[...] = x1
    ms = jnp.mean(x1 * x1, axis=-1, keepdims=True)
    h2_ref[...] = (x1 * lax.rsqrt(ms + EPS) * g_ref[...]).astype(BF16)


def _outproj(x2d, yf, ya, w_out_bf, g2):
    t = x2d.shape[0]
    tm = TOKEN_TILE
    return pl.pallas_call(
        _outproj_kernel,
        grid=(t // tm,),
        in_specs=[
            pl.BlockSpec((tm, D_MODEL), lambda i: (i, 0)),
            pl.BlockSpec((tm, FOURIER_WIDTH), lambda i: (i, 0)),
            pl.BlockSpec((tm, ATTN_WIDTH), lambda i: (i, 0)),
            pl.BlockSpec((D_MODEL, D_MODEL), lambda i: (0, 0)),
            pl.BlockSpec((1, D_MODEL), lambda i: (0, 0)),
        ],
        out_specs=[
            pl.BlockSpec((tm, D_MODEL), lambda i: (i, 0)),
            pl.BlockSpec((tm, D_MODEL), lambda i: (i, 0)),
        ],
        out_shape=[
            jax.ShapeDtypeStruct((t, D_MODEL), F32),
            jax.ShapeDtypeStruct((t, D_MODEL), BF16),
        ],
        compiler_params=_cparams(("arbitrary",)),
        name="outproj",
    )(x2d, yf, ya, w_out_bf, g2.reshape(1, D_MODEL))


def _ffn_kernel(h_ref, x1_ref, wg_ref, wu_ref, wd_ref, g_ref, o_ref, acc_ref):
    f = pl.program_id(1)

    @pl.when(f == 0)
    def _():
        acc_ref[...] = x1_ref[...]

    h = h_ref[...]
    gate = jnp.dot(h, wg_ref[...], preferred_element_type=F32)
    up = jnp.dot(h, wu_ref[...], preferred_element_type=F32)
    a = (gate * (1.0 / (1.0 + jnp.exp(-gate))) * up).astype(BF16)
    acc_ref[...] += jnp.dot(a, wd_ref[...], preferred_element_type=F32)

    @pl.when(f == pl.num_programs(1) - 1)
    def _():
        x2 = acc_ref[...]
        ms = jnp.mean(x2 * x2, axis=-1, keepdims=True)
        o_ref[...] = x2 * lax.rsqrt(ms + EPS) * g_ref[...]


def _ffn(h2, x1, wg_bf, wu_bf, wd_bf, gf):
    t = h2.shape[0]
    tm = TOKEN_TILE
    tf = FF_TILE
    return pl.pallas_call(
        _ffn_kernel,
        grid=(t // tm, D_FF // tf),
        in_specs=[
            pl.BlockSpec((tm, D_MODEL), lambda i, f: (i, 0)),
            pl.BlockSpec((tm, D_MODEL), lambda i, f: (i, 0)),
            pl.BlockSpec((D_MODEL, tf), lambda i, f: (0, f)),
            pl.BlockSpec((D_MODEL, tf), lambda i, f: (0, f)),
            pl.BlockSpec((tf, D_MODEL), lambda i, f: (f, 0)),
            pl.BlockSpec((1, D_MODEL), lambda i, f: (0, 0)),
        ],
        out_specs=pl.BlockSpec((tm, D_MODEL), lambda i, f: (i, 0)),
        out_shape=jax.ShapeDtypeStruct((t, D_MODEL), F32),
        scratch_shapes=[pltpu.VMEM((tm, D_MODEL), F32)],
        compiler_params=_cparams(("arbitrary", "arbitrary")),
        name="ffn",
    )(h2, x1, wg_bf, wu_bf, wd_bf, gf.reshape(1, D_MODEL))


def _encoder(x, params):
    b, s, _ = x.shape
    t = b * s
    n1 = s // SEQ_MINOR
    x2d = x.reshape(t, D_MODEL)

    pq, qt, k, vt = _inproj(x2d, params["norm1_g"], params["w_in"], params["ab"])
    y1 = _fourier1(pq.reshape(2, b, n1, SEQ_MINOR, FOURIER_WIDTH), n1)
    yf = _fourier2(y1, n1).reshape(t, FOURIER_WIDTH)
    ya = _attention(qt, k, vt, params["wtab"], params["cfar"], params["lam"], params["subln_g"], b, s)
    x1, h2 = _outproj(x2d, yf, ya, params["w_out"], params["norm2_g"])
    out = _ffn(h2, x1, params["w_gate"], params["w_up"], params["w_down"], params["final_g"])
    return out.reshape(b, s, D_MODEL)


def kernel(x_prompt, x_sample, norm1_g, w_in, w_fourier, lambda_q1, lambda_k1, lambda_q2, lambda_k2,
           subln_g, w_out, norm2_g, w_gate, w_up, w_down, rel_bias, final_g):
    wtab, cfar = _bias_tables(rel_bias)
    params = {
        "norm1_g": norm1_g[0].astype(F32),
        "w_in": w_in[0].astype(BF16),
        "ab": _fourier_weights(w_fourier[0].astype(F32)),
        "lam": jnp.stack([lambda_q1[0], lambda_k1[0], lambda_q2[0], lambda_k2[0]]).astype(F32),
        "subln_g": subln_g[0].astype(F32),
        "w_out": w_out[0].astype(BF16),
        "norm2_g": norm2_g[0].astype(F32),
        "w_gate": w_gate[0].astype(BF16),
        "w_up": w_up[0].astype(BF16),
        "w_down": w_down[0].astype(BF16),
        "wtab": wtab,
        "cfar": cfar,
        "final_g": final_g.astype(F32),
    }
    return (_encoder(x_prompt, params), _encoder(x_sample, params))
```

```python
import functools
import math

import numpy as np
import jax
import jax.numpy as jnp
from jax import lax
from jax.experimental import pallas as pl
from jax.experimental.pallas import tpu as pltpu

F32 = jnp.float32
BF16 = jnp.bfloat16

D_MODEL = 2048
FOURIER_WIDTH = 1024
FOURIER_GROUP = 128
N_FGROUPS = 8
ATTN_WIDTH = 1024
ATTN_HEAD = 128
N_HEADS = 8
HALF_DIM = 64
D_FF = 5632
NUM_BUCKETS = 32
MAX_DISTANCE = 128
EPS = 1e-6
LAM_INIT = 0.8 - 0.6 * math.exp(-0.3 * 0)

SEQ_MINOR = 128
TOKEN_TILE = 512
FF_TILE = 512
ATTN_Q_TILE = 1024
ATTN_Q_TILE_SHORT = 2048
ATTN_KV_TILE = 1024
ATTN_EXP_GUARD = 64.0
ATTN_LEVEL_ROWS = 16
ATTN_QK_AHEAD_NEAR = 4
ATTN_LANE_CHUNK = 256
LOG2E = math.log2(math.e)
NEG_BIG = -1e30
VMEM_LIMIT = 56 * 1024 * 1024


def _cparams(sem):
    return pltpu.CompilerParams(dimension_semantics=sem, vmem_limit_bytes=VMEM_LIMIT)


def _dft128_tables():
    n = np.arange(FOURIER_GROUP)
    ang = 2.0 * np.pi * ((n[:, None] * n[None, :]) % FOURIER_GROUP) / FOURIER_GROUP
    return np.stack([np.cos(ang), np.sin(ang)]).astype(np.float32)


def _stage1_matrix(n1):
    n = np.arange(n1)
    ang = 2.0 * np.pi * ((n[:, None] * n[None, :]) % n1) / n1
    c, s = np.cos(ang), np.sin(ang)
    return np.block([[c, -s], [-s, -c]]).astype(np.float32)


def _stage2_matrices(n1):
    s_len = n1 * SEQ_MINOR
    s1p = np.arange(n1)[:, None, None]
    s2p = np.arange(SEQ_MINOR)[None, :, None]
    s2 = np.arange(SEQ_MINOR)[None, None, :]
    ang = 2.0 * np.pi * ((s2 * (s1p + n1 * s2p)) % s_len) / s_len
    scale = 1.0 / math.sqrt(s_len * FOURIER_GROUP)
    return (np.concatenate([np.cos(ang), np.sin(ang)], axis=-1) * scale).astype(np.float32)


def _fourier_weight_kernel(dft_ref, w_ref, ab_ref):
    for g in range(N_FGROUPS):
        w = w_ref[g]
        a = jnp.dot(dft_ref[0], w, preferred_element_type=F32, precision=lax.Precision.HIGHEST)
        b = jnp.dot(dft_ref[1], w, preferred_element_type=F32, precision=lax.Precision.HIGHEST)
        ab_ref[g, :, :FOURIER_GROUP] = a.astype(BF16)
        ab_ref[g, :, FOURIER_GROUP:] = b.astype(BF16)


def _fourier_weights(w_f):
    return pl.pallas_call(
        _fourier_weight_kernel,
        out_shape=jax.ShapeDtypeStruct((N_FGROUPS, FOURIER_GROUP, 2 * FOURIER_GROUP), BF16),
        name="fourier_weights",
    )(jnp.asarray(_dft128_tables()), w_f)


def _inproj_kernel(x_ref, g_ref, w_ref, ab_ref, pq_ref, qt_ref, k_ref, vt_ref):
    x = x_ref[...]
    ms = jnp.mean(x * x, axis=-1, keepdims=True)
    h = (x * lax.rsqrt(ms + EPS) * g_ref[...]).astype(BF16)

    def proj(col):
        return jnp.dot(h, w_ref[:, col:col + ATTN_WIDTH], preferred_element_type=F32)

    ub = proj(0).astype(BF16)
    for g in range(N_FGROUPS):
        lo, hi = g * FOURIER_GROUP, (g + 1) * FOURIER_GROUP
        pq = jnp.dot(ub[:, lo:hi], ab_ref[g], preferred_element_type=F32)
        pq_ref[0, :, lo:hi] = pq[:, :FOURIER_GROUP].astype(BF16)
        pq_ref[1, :, lo:hi] = pq[:, FOURIER_GROUP:].astype(BF16)
    qt_ref[0] = (proj(FOURIER_WIDTH) * (HALF_DIM ** -0.5 * LOG2E)).T.astype(BF16)
    k_ref[...] = proj(FOURIER_WIDTH + ATTN_WIDTH).astype(BF16)
    vt_ref[0] = proj(FOURIER_WIDTH + 2 * ATTN_WIDTH).T.astype(BF16)


def _inproj(x2d, g, w_in_bf, ab):
    t = x2d.shape[0]
    tm = TOKEN_TILE
    nt = t // tm
    return pl.pallas_call(
        _inproj_kernel,
        grid=(nt,),
        in_specs=[
            pl.BlockSpec((tm, D_MODEL), lambda i: (i, 0)),
            pl.BlockSpec((1, D_MODEL), lambda i: (0, 0)),
            pl.BlockSpec((D_MODEL, FOURIER_WIDTH + 3 * ATTN_WIDTH), lambda i: (0, 0), pipeline_mode=pl.Buffered(1)),
            pl.BlockSpec((N_FGROUPS, FOURIER_GROUP, 2 * FOURIER_GROUP), lambda i: (0, 0, 0)),
        ],
        out_specs=[
            pl.BlockSpec((2, tm, FOURIER_WIDTH), lambda i: (0, i, 0)),
            pl.BlockSpec((1, ATTN_WIDTH, tm), lambda i: (i, 0, 0)),
            pl.BlockSpec((tm, ATTN_WIDTH), lambda i: (i, 0)),
            pl.BlockSpec((1, ATTN_WIDTH, tm), lambda i: (i, 0, 0)),
        ],
        out_shape=[
            jax.ShapeDtypeStruct((2, t, FOURIER_WIDTH), BF16),
            jax.ShapeDtypeStruct((nt, ATTN_WIDTH, tm), BF16),
            jax.ShapeDtypeStruct((t, ATTN_WIDTH), BF16),
            jax.ShapeDtypeStruct((nt, ATTN_WIDTH, tm), BF16),
        ],
        compiler_params=_cparams(("arbitrary",)),
        name="inproj",
    )(x2d, g.reshape(1, D_MODEL), w_in_bf, ab)


def _fourier1_kernel(x_ref, d_ref, o_ref, *, n1, rows):
    x = pltpu.einshape("abc->bac", x_ref[...].reshape(2 * n1, rows, FOURIER_WIDTH))
    d = d_ref[...]
    for r in range(rows):
        y = jnp.dot(d, x[r], preferred_element_type=F32)
        o_ref[0, r] = y[:n1].astype(BF16)
        o_ref[1, r] = y[n1:].astype(BF16)


def _fourier1(pq5, n1):
    b = pq5.shape[1]
    rows = 16 if n1 >= 64 else 64
    d1 = jnp.asarray(_stage1_matrix(n1)).astype(BF16)
    return pl.pallas_call(
        functools.partial(_fourier1_kernel, n1=n1, rows=rows),
        grid=(b, SEQ_MINOR // rows),
        in_specs=[
            pl.BlockSpec((2, None, n1, rows, FOURIER_WIDTH), lambda i, j: (0, i, 0, j, 0)),
            pl.BlockSpec((2 * n1, 2 * n1), lambda i, j: (0, 0)),
        ],
        out_specs=pl.BlockSpec((None, 2, rows, n1, FOURIER_WIDTH), lambda i, j: (i, 0, j, 0, 0)),
        out_shape=jax.ShapeDtypeStruct((b, 2, SEQ_MINOR, n1, FOURIER_WIDTH), BF16),
        compiler_params=_cparams(("arbitrary", "arbitrary")),
        name="fourier_stage1",
    )(pq5, d1)


def _fourier2_kernel(x_ref, m_ref, o_ref, *, rows):
    x = pltpu.einshape("abc->bac", x_ref[...].reshape(2 * SEQ_MINOR, rows, FOURIER_WIDTH))
    ys = [jnp.dot(m_ref[i], x[i], preferred_element_type=F32).astype(BF16) for i in range(rows)]
    o_ref[...] = pltpu.einshape("abc->bac", jnp.stack(ys))


def _fourier2(y1, n1):
    b = y1.shape[0]
    rows = 16
    m2 = jnp.asarray(_stage2_matrices(n1)).astype(BF16)
    return pl.pallas_call(
        functools.partial(_fourier2_kernel, rows=rows),
        grid=(b, n1 // rows),
        in_specs=[
            pl.BlockSpec((None, 2, SEQ_MINOR, rows, FOURIER_WIDTH), lambda i, j: (i, 0, 0, j, 0)),
            pl.BlockSpec((rows, SEQ_MINOR, 2 * SEQ_MINOR), lambda i, j: (j, 0, 0)),
        ],
        out_specs=pl.BlockSpec((None, SEQ_MINOR, rows, FOURIER_WIDTH), lambda i, j: (i, 0, j, 0)),
        out_shape=jax.ShapeDtypeStruct((b, SEQ_MINOR, n1, FOURIER_WIDTH), BF16),
        compiler_params=_cparams(("arbitrary", "arbitrary")),
        name="fourier_stage2",
    )(y1, m2)


def _t5_bucket(rel):
    nb = NUM_BUCKETS // 2
    max_exact = nb // 2
    ret = (rel > 0).astype(jnp.int32) * nb
    n = jnp.abs(rel)
    nf = jnp.maximum(n, 1).astype(jnp.float32)
    large = max_exact + (jnp.log(nf / max_exact) / math.log(MAX_DISTANCE / max_exact)
                         * (nb - max_exact)).astype(jnp.int32)
    large = jnp.minimum(large, nb - 1)
    return ret + jnp.where(n < max_exact, n, large)


def _bias_tables(rel_bias):
    tq, w = max(ATTN_Q_TILE, ATTN_Q_TILE_SHORT), ATTN_LANE_CHUNK
    mp = np.arange(2 * w)
    m = np.where(mp < w, mp, mp - 2 * w)
    d = (np.arange(3)[:, None] - 1) * w - m[None, :]
    tbl = rel_bias.astype(F32) * LOG2E
    g = tbl[_t5_bucket(jnp.asarray(d, jnp.int32))]
    wtab = jnp.transpose(g, (2, 0, 1))[:, :, None, :]
    far = tbl[_t5_bucket(jnp.asarray([-2 * MAX_DISTANCE, 2 * MAX_DISTANCE], jnp.int32))]
    cfar = jnp.broadcast_to(far.T[:, :, None, None], (N_HEADS, 2, 1, tq))
    return wtab, cfar


def _attn_kernel(lam_ref, qt_ref, k_ref, vt_ref, w_ref, cfar_ref, g_ref, o_ref, m_sc, l_sc, acc_sc, bias_sc, ex_sc,
                 *, nkv, tq, tk):
    qi = pl.program_id(2)
    r = tq // tk
    w = ATTN_LANE_CHUNK
    chunks = tq // w
    pieces = tk // TOKEN_TILE

    @pl.when((pl.program_id(1) == 0) & (qi == 0))
    def _():
        for u in range(3):
            x = jnp.broadcast_to(w_ref[u], (w, 2 * w))
            bias_sc[u] = pltpu.roll(x, 0, 1, stride=1, stride_axis=0)[:, :w]

    def sub_bias(tile, a, lc):
        dist = (tile - 1) * tk + (a - lc) * w
        if abs(dist) >= w - 1 + MAX_DISTANCE:
            return None, cfar_ref[0 if dist < 0 else 1][:, lc * w:(lc + 1) * w]
        return dist // w + 1, None

    row = lax.broadcasted_iota(jnp.int32, (ATTN_HEAD, w), 0)
    qts = []
    for lc in range(chunks):
        q = qt_ref[(lc * w) // TOKEN_TILE][:, (lc * w) % TOKEN_TILE:(lc * w) % TOKEN_TILE + w]
        zero = jnp.zeros_like(q)
        qts.append((jnp.where(row < HALF_DIM, q, zero), jnp.where(row >= HALF_DIM, q, zero)))

    lo = jnp.maximum(qi * r - 1, 0)
    hi = jnp.minimum((qi + 1) * r + 1, nkv)

    def logits(j, c, lc):
        kj = k_ref[pl.ds(pl.multiple_of(j * tk, tk), tk), :]
        return jnp.dot(kj, qts[lc][c], preferred_element_type=F32)

    def pv(j, p):
        pb = p.astype(BF16)
        out = jnp.dot(vt_ref[j * pieces], pb[:TOKEN_TILE], preferred_element_type=F32)
        for i in range(1, pieces):
            out += jnp.dot(vt_ref[j * pieces + i], pb[i * TOKEN_TILE:(i + 1) * TOKEN_TILE],
                           preferred_element_type=F32)
        return out

    def safe_block(j, tile, c_row):
        ss = [[logits(j, c, lc) for lc in range(chunks)] for c in range(2)]
        for c in range(2):
            for lc in range(chunks):
                sl = slice(lc * w, (lc + 1) * w)
                s = ss[c][lc]
                m_old = m_sc[c, :, sl]
                if tile is not None:
                    ts = []
                    for a in range(tk // w):
                        u, side = sub_bias(tile, a, lc)
                        sa = jnp.maximum(s[a * w:(a + 1) * w], NEG_BIG)
                        ts.append(sa + (side if u is None else bias_sc[u]))
                    s = jnp.concatenate(ts, axis=0)
                    m_new = jnp.maximum(m_old, jnp.max(s, axis=0, keepdims=True))
                    p = jnp.exp2(s - m_new)
                else:
                    cr = c_row[:, sl]
                    m_new = jnp.maximum(m_old, jnp.max(s, axis=0, keepdims=True) + cr)
                    p = jnp.exp2(s - (m_new - cr))
                alpha = jnp.exp2(m_old - m_new)
                l_sc[c, :, sl] = alpha * l_sc[c, :, sl] + jnp.sum(p, axis=0, keepdims=True)
                acc_sc[c, :, sl] = alpha * acc_sc[c, :, sl] + pv(j, p)
                m_sc[c, :, sl] = m_new

    def fast_block(j, tile, c_row):
        order = [(c, lc) for c in range(2) for lc in range(chunks)]
        ahead = len(order) if tile is None else ATTN_QK_AHEAD_NEAR
        pending = {key: logits(j, *key) for key in order[:ahead]}
        for idx, (c, lc) in enumerate(order):
            if idx + ahead < len(order):
                nxt = order[idx + ahead]
                pending[nxt] = logits(j, *nxt)
            sl = slice(lc * w, (lc + 1) * w)
            s = pending.pop((c, lc))
            m_old = m_sc[c, :, sl]
            if tile is not None:
                es = []
                for a in range(tk // w):
                    rows = slice(a * w, (a + 1) * w)
                    u, side = sub_bias(tile, a, lc)
                    if u is None:
                        es.append(s[rows] - (m_old - side))
                    else:
                        es.append((s[rows] - m_old) + bias_sc[u])
                e = jnp.concatenate(es, axis=0)
                p = jnp.exp2(e)
                bm = jnp.max(e, axis=0, keepdims=True) + m_old
            else:
                cr = c_row[:, sl]
                p = jnp.exp2(s - (m_old - cr))
                bm = jnp.max(s, axis=0, keepdims=True) + cr
            m_new = jnp.maximum(m_old, bm)
            alpha = jnp.exp2(m_old - m_new)
            ex_sc[c, :, sl] = jnp.maximum(ex_sc[c, :, sl], bm - m_old)
            l_sc[c, :, sl] = alpha * (l_sc[c, :, sl] + jnp.sum(p, axis=0, keepdims=True))
            acc_sc[c, :, sl] = alpha * (acc_sc[c, :, sl] + pv(j, p))
            m_sc[c, :, sl] = m_new

    def run(block_fn, pair_far):
        def far(c_row):
            def body(j, carry):
                block_fn(j, None, c_row)
                return carry
            return body

        def near():
            for t in range(r + 2):
                j_t = qi * r - 1 + t

                @pl.when((j_t >= 0) & (j_t < nkv))
                def _():
                    block_fn(j_t, t, None)

        def far_pair(c_row):
            def body(i, first):
                block_fn(first + 2 * i, None, c_row)
                block_fn(first + 2 * i + 1, None, c_row)
                return first
            return body

        if pair_far:
            n_lo = lo // 2
            lax.fori_loop(0, n_lo, far_pair(cfar_ref[0]), 0)
            lax.fori_loop(2 * n_lo, lo, far(cfar_ref[0]), 0)
            near()
            n_hi = (nkv - hi) // 2
            lax.fori_loop(0, n_hi, far_pair(cfar_ref[1]), hi)
            lax.fori_loop(hi + 2 * n_hi, nkv, far(cfar_ref[1]), 0)
        else:
            lax.fori_loop(0, lo, far(cfar_ref[0]), 0)
            near()
            lax.fori_loop(hi, nkv, far(cfar_ref[1]), 0)

    l_sc[...] = jnp.zeros(l_sc.shape, F32)
    acc_sc[...] = jnp.zeros(acc_sc.shape, F32)
    ex_sc[...] = jnp.zeros(ex_sc.shape, F32)

    k0 = k_ref[0:ATTN_LEVEL_ROWS, :]

    @pl.when(qi == 0)
    def _():
        for c in range(2):
            for lc in range(chunks):
                sl = slice(lc * w, (lc + 1) * w)
                s0 = jnp.maximum(jnp.dot(k0, qts[lc][c], preferred_element_type=F32), NEG_BIG)
                u, side = sub_bias(1, 0, lc)
                s0 = s0 + (side if u is None else bias_sc[u, 0:ATTN_LEVEL_ROWS, :])
                m_sc[c, :, sl] = jnp.max(s0, axis=0, keepdims=True)

    @pl.when(qi != 0)
    def _():
        for c in range(2):
            for lc in range(chunks):
                sl = slice(lc * w, (lc + 1) * w)
                s0 = jnp.dot(k0, qts[lc][c], preferred_element_type=F32)
                m_sc[c, :, sl] = jnp.max(s0, axis=0, keepdims=True) + cfar_ref[0][:, sl]

    run(fast_block, True)

    @pl.when(jnp.max(ex_sc[...]) > ATTN_EXP_GUARD)
    def _():
        m_sc[...] = jnp.full(m_sc.shape, NEG_BIG, F32)
        l_sc[...] = jnp.zeros(l_sc.shape, F32)
        acc_sc[...] = jnp.zeros(acc_sc.shape, F32)
        run(safe_block, False)

    lam = (jnp.exp(jnp.sum(lam_ref[0:1, :] * lam_ref[1:2, :], axis=-1, keepdims=True))
           - jnp.exp(jnp.sum(lam_ref[2:3, :] * lam_ref[3:4, :], axis=-1, keepdims=True))
           + LAM_INIT)
    o = acc_sc[0] / l_sc[0] - lam * (acc_sc[1] / l_sc[1])
    ms = jnp.mean(o * o, axis=0, keepdims=True)
    o = o * lax.rsqrt(ms + EPS) * g_ref[...]
    o = o * (1.0 - LAM_INIT)
    o_ref[...] = o.T.astype(BF16)


def _attention(qt, k, vt, wtab, cfar, lam_params, subln_g, b, s):
    tq = ATTN_Q_TILE_SHORT if s <= ATTN_Q_TILE_SHORT else ATTN_Q_TILE
    tk, w = ATTN_KV_TILE, ATTN_LANE_CHUNK
    assert s % tq == 0 and s % tk == 0 and tq % tk == 0 and tk % TOKEN_TILE == 0
    nq = s // tq
    nkv = s // tk
    t = b * s
    return pl.pallas_call(
        functools.partial(_attn_kernel, nkv=nkv, tq=tq, tk=tk),
        grid=(N_HEADS, b, nq),
        in_specs=[
            pl.BlockSpec((4, HALF_DIM), lambda h, bi, qi: (0, 0)),
            pl.BlockSpec((tq // TOKEN_TILE, ATTN_HEAD, TOKEN_TILE), lambda h, bi, qi: (bi * nq + qi, h, 0)),
            pl.BlockSpec((s, ATTN_HEAD), lambda h, bi, qi: (bi, h)),
            pl.BlockSpec((s // TOKEN_TILE, ATTN_HEAD, TOKEN_TILE), lambda h, bi, qi: (bi, h, 0)),
            pl.BlockSpec((None, 3, 1, 2 * w), lambda h, bi, qi: (h, 0, 0, 0)),
            pl.BlockSpec((None, 2, 1, tq), lambda h, bi, qi: (h, 0, 0, 0)),
            pl.BlockSpec((ATTN_HEAD, 1), lambda h, bi, qi: (0, 0)),
        ],
        out_specs=pl.BlockSpec((tq, ATTN_HEAD), lambda h, bi, qi: (bi * nq + qi, h)),
        out_shape=jax.ShapeDtypeStruct((t, ATTN_WIDTH), BF16),
        scratch_shapes=[
            pltpu.VMEM((2, 1, tq), F32),
            pltpu.VMEM((2, 1, tq), F32),
            pltpu.VMEM((2, ATTN_HEAD, tq), F32),
            pltpu.VMEM((3, w, w), F32),
            pltpu.VMEM((2, 1, tq), F32),
        ],
        compiler_params=_cparams(("arbitrary", "arbitrary", "arbitrary")),
        name="diff_attention",
    )(lam_params, qt, k, vt, wtab, cfar, subln_g.reshape(ATTN_HEAD, 1))


def _ffn_kernel(x_ref, yf_ref, ya_ref, wo_ref, g2_ref, wg_ref, wu_ref, wd_ref, g_ref, o_ref, acc_ref, h_ref):
    f = pl.program_id(1)

    @pl.when(f == 0)
    def _():
        y = (jnp.dot(yf_ref[...], wo_ref[:FOURIER_WIDTH, :], preferred_element_type=F32)
             + jnp.dot(ya_ref[...], wo_ref[FOURIER_WIDTH:, :], preferred_element_type=F32))
        x1 = x_ref[...] + y
        acc_ref[...] = x1
        ms1 = jnp.mean(x1 * x1, axis=-1, keepdims=True)
        h_ref[...] = (x1 * lax.rsqrt(ms1 + EPS) * g2_ref[...]).astype(BF16)

    h = h_ref[...]
    gate = jnp.dot(h, wg_ref[...], preferred_element_type=F32)
    up = jnp.dot(h, wu_ref[...], preferred_element_type=F32)
    a = (gate * (1.0 / (1.0 + jnp.exp(-gate))) * up).astype(BF16)
    acc_ref[...] += jnp.dot(a, wd_ref[...], preferred_element_type=F32)

    @pl.when(f == pl.num_programs(1) - 1)
    def _():
        x2 = acc_ref[...]
        ms = jnp.mean(x2 * x2, axis=-1, keepdims=True)
        o_ref[...] = x2 * lax.rsqrt(ms + EPS) * g_ref[...]


def _ffn(x2d, yf, ya, w_out_bf, g2, wg_bf, wu_bf, wd_bf, gf):
    t = x2d.shape[0]
    tm = TOKEN_TILE
    tf = FF_TILE
    return pl.pallas_call(
        _ffn_kernel,
        grid=(t // tm, D_FF // tf),
        in_specs=[
            pl.BlockSpec((tm, D_MODEL), lambda i, f: (i, 0)),
            pl.BlockSpec((tm, FOURIER_WIDTH), lambda i, f: (i, 0)),
            pl.BlockSpec((tm, ATTN_WIDTH), lambda i, f: (i, 0)),
            pl.BlockSpec((D_MODEL, D_MODEL), lambda i, f: (0, 0), pipeline_mode=pl.Buffered(1)),
            pl.BlockSpec((1, D_MODEL), lambda i, f: (0, 0)),
            pl.BlockSpec((D_MODEL, tf), lambda i, f: (0, f)),
            pl.BlockSpec((D_MODEL, tf), lambda i, f: (0, f)),
            pl.BlockSpec((tf, D_MODEL), lambda i, f: (f, 0)),
            pl.BlockSpec((1, D_MODEL), lambda i, f: (0, 0)),
        ],
        out_specs=pl.BlockSpec((tm, D_MODEL), lambda i, f: (i, 0)),
        out_shape=jax.ShapeDtypeStruct((t, D_MODEL), F32),
        scratch_shapes=[pltpu.VMEM((tm, D_MODEL), F32), pltpu.VMEM((tm, D_MODEL), BF16)],
        compiler_params=_cparams(("arbitrary", "arbitrary")),
        name="outproj_ffn",
    )(x2d, yf, ya, w_out_bf, g2.reshape(1, D_MODEL), wg_bf, wu_bf, wd_bf, gf.reshape(1, D_MODEL))


def _encoder(x, params):
    b, s, _ = x.shape
    t = b * s
    n1 = s // SEQ_MINOR
    x2d = x.reshape(t, D_MODEL)

    pq, qt, k, vt = _inproj(x2d, params["norm1_g"], params["w_in"], params["ab"])
    y1 = _fourier1(pq.reshape(2, b, n1, SEQ_MINOR, FOURIER_WIDTH), n1)
    yf = _fourier2(y1, n1).reshape(t, FOURIER_WIDTH)
    ya = _attention(qt, k, vt, params["wtab"], params["cfar"], params["lam"], params["subln_g"], b, s)
    out = _ffn(x2d, yf, ya, params["w_out"], params["norm2_g"],
               params["w_gate"], params["w_up"], params["w_down"], params["final_g"])
    return out.reshape(b, s, D_MODEL)


def kernel(x_prompt, x_sample, norm1_g, w_in, w_fourier, lambda_q1, lambda_k1, lambda_q2, lambda_k2,
           subln_g, w_out, norm2_g, w_gate, w_up, w_down, rel_bias, final_g):
    wtab, cfar = _bias_tables(rel_bias)
    params = {
        "norm1_g": norm1_g[0].astype(F32),
        "w_in": w_in[0].astype(BF16),
        "ab": _fourier_weights(w_fourier[0].astype(F32)),
        "lam": jnp.stack([lambda_q1[0], lambda_k1[0], lambda_q2[0], lambda_k2[0]]).astype(F32),
        "subln_g": subln_g[0].astype(F32),
        "w_out": w_out[0].astype(BF16),
        "norm2_g": norm2_g[0].astype(F32),
        "w_gate": w_gate[0].astype(BF16),
        "w_up": w_up[0].astype(BF16),
        "w_down": w_down[0].astype(BF16),
        "wtab": wtab,
        "cfar": cfar,
        "final_g": final_g.astype(F32),
    }
    return (_encoder(x_prompt, params), _encoder(x_sample, params))
```
